```python
import jax, jax.numpy as jnp
from jax import lax
import numpy as np

D_MODEL = 2048
BATCH = 2
SEQ = 8192
DEPTH = 4

GRID_W = 64
CTX_LEN = 256
N_MIXERS = 3
N_A = (DEPTH + 2) // 3
N_B = (DEPTH + 1) // 3
N_C = DEPTH // 3
CHUNK = 128
GMLP_WIDTH = D_MODEL
GMLP_GROUP_DIM = 128
GMLP_GROUPS = GMLP_WIDTH // GMLP_GROUP_DIM
NA_HEAD_DIM = 128
NA_HEADS = D_MODEL // NA_HEAD_DIM
NA_MAX_ROWS = 8
NA_COLS = 16
CONV_WIDTH = 31
FFN_HIDDEN = (((8 * D_MODEL + 2) // 3 + 255) // 256) * 256
EPS = 1e-6
NEG_INF = -1e30

kernel_name = "hybrid_interleaved_dit_backbone"


def rmsnorm(x, g):
    xf = x.astype(jnp.float32)
    y = xf * lax.rsqrt(jnp.mean(xf * xf, axis=-1, keepdims=True) + EPS)
    return (y * g.astype(jnp.float32)).astype(x.dtype)


def layernorm(x, g, b):
    xf = x.astype(jnp.float32)
    mu = jnp.mean(xf, axis=-1, keepdims=True)
    xc = xf - mu
    var = jnp.mean(xc * xc, axis=-1, keepdims=True)
    return (xc * lax.rsqrt(var + EPS) * g.astype(jnp.float32) + b.astype(jnp.float32)).astype(x.dtype)


def modulate(h, shift, scale):
    return h * (1 + scale) + shift


def swiglu(h, w1, w3, w2):
    return (jax.nn.silu(h @ w1) * (h @ w3)) @ w2


def gmlp_chunk_mixer(h, w_in, ln_g, ln_b, w_s, b_s, w_out):
    bsz, L, _ = h.shape
    z = jax.nn.gelu(h @ w_in)
    u, v = jnp.split(z, 2, axis=-1)
    v = layernorm(v, ln_g, ln_b)
    v = v.reshape(bsz, L // CHUNK, CHUNK, GMLP_GROUPS, GMLP_GROUP_DIM)
    v = jnp.einsum('gpq,bnqgc->bnpgc', w_s, v) + b_s.T[:, :, None]
    v = v.reshape(bsz, L, GMLP_WIDTH)
    return (u * v) @ w_out


def _heads(t):
    return t.reshape(t.shape[0], t.shape[1], NA_HEADS, NA_HEAD_DIM).transpose(0, 2, 1, 3)


def neighbourhood_attention(h_lat, h_ctx, w_qkv, rpb, w_out, need_ctx_out):
    bsz, L, _ = h_lat.shape
    rows = L // GRID_W
    kr = min(NA_MAX_ROWS, rows)
    scale = NA_HEAD_DIM ** -0.5

    q, k, v = jnp.split(h_lat @ w_qkv, 3, axis=-1)
    qc, kc, vc = jnp.split(h_ctx @ w_qkv, 3, axis=-1)
    q, k, v = _heads(q), _heads(k), _heads(v)
    qc, kc, vc = _heads(qc), _heads(kc), _heads(vc)
    grid = (bsz, NA_HEADS, rows, GRID_W, NA_HEAD_DIM)
    qg, kg, vg = q.reshape(grid), k.reshape(grid), v.reshape(grid)

    cols = jnp.arange(GRID_W)
    c_start = jnp.clip(cols - NA_COLS // 2, 0, GRID_W - NA_COLS)
    col_in = (cols[None, :] >= c_start[:, None]) & (cols[None, :] < c_start[:, None] + NA_COLS)
    mask = jnp.broadcast_to(col_in[:, None, :], (GRID_W, kr, GRID_W)).reshape(GRID_W, kr * GRID_W)
    dc_idx = jnp.clip(cols[None, :] - cols[:, None] + NA_COLS - 1, 0, 2 * NA_COLS - 2)

    def row_block(r):
        r_start = jnp.clip(r - kr // 2, 0, rows - kr)
        q_r = lax.dynamic_index_in_dim(qg, r, axis=2, keepdims=False)
        k_r = lax.dynamic_slice_in_dim(kg, r_start, kr, axis=2).reshape(bsz, NA_HEADS, kr * GRID_W, NA_HEAD_DIM)
        v_r = lax.dynamic_slice_in_dim(vg, r_start, kr, axis=2).reshape(bsz, NA_HEADS, kr * GRID_W, NA_HEAD_DIM)
        dr_idx = r_start + jnp.arange(kr) - r + (NA_MAX_ROWS - 1)
        bias = rpb[:, dr_idx][:, :, dc_idx]
        bias = bias.transpose(0, 2, 1, 3).reshape(NA_HEADS, GRID_W, kr * GRID_W)
        s_lat = jnp.einsum('bhqd,bhkd->bhqk', q_r, k_r).astype(jnp.float32) * scale + bias.astype(jnp.float32)
        s_lat = jnp.where(mask, s_lat, NEG_INF)
        s_ctx = jnp.einsum('bhqd,bhkd->bhqk', q_r, kc).astype(jnp.float32) * scale
        p = jax.nn.softmax(jnp.concatenate([s_lat, s_ctx], axis=-1), axis=-1).astype(v.dtype)
        return (jnp.einsum('bhqk,bhkd->bhqd', p[..., :kr * GRID_W], v_r)
                + jnp.einsum('bhqk,bhkd->bhqd', p[..., kr * GRID_W:], vc))

    o = lax.map(row_block, jnp.arange(rows))
    o = o.transpose(1, 0, 3, 2, 4).reshape(bsz, L, D_MODEL)
    out_lat = o @ w_out
    if not need_ctx_out:
        return out_lat, None
    s_c = jnp.einsum('bhqd,bhkd->bhqk', qc, kc).astype(jnp.float32) * scale
    p_c = jax.nn.softmax(s_c, axis=-1).astype(vc.dtype)
    oc = jnp.einsum('bhqk,bhkd->bhqd', p_c, vc).transpose(0, 2, 1, 3).reshape(bsz, h_ctx.shape[1], D_MODEL)
    return out_lat, oc @ w_out


def conformer_conv(h, w_pw1, w_dw, b_dw, ln_g, ln_b, w_pw2):
    a, g = jnp.split(h @ w_pw1, 2, axis=-1)
    y = a * jax.nn.sigmoid(g)
    y = lax.conv_general_dilated(
        y, w_dw[:, None, :], window_strides=(1,),
        padding=[(CONV_WIDTH // 2, CONV_WIDTH // 2)],
        dimension_numbers=('NWC', 'WIO', 'NWC'),
        feature_group_count=D_MODEL) + b_dw
    y = jax.nn.silu(layernorm(y, ln_g, ln_b))
    return y @ w_pw2


def setup_inputs(seed: int = 0) -> dict:
    key = jax.random.key(seed)
    keys = iter(jax.random.split(key, 40))
    D, F, E = D_MODEL, FFN_HIDDEN, GMLP_WIDTH

    def nrm(shape, scale):
        return jax.random.normal(next(keys), shape, jnp.float32) * scale

    def gain(shape):
        return 1.0 + nrm(shape, 0.02)

    return {
        "x": nrm((BATCH, SEQ, D), 1.0),
        "c": nrm((BATCH, D), 1.0),
        "ctx": nrm((BATCH, CTX_LEN, D), 1.0),
        "c_ctx": nrm((D,), 1.0),
        "ada_w": nrm((DEPTH, D, 6 * D), 0.5 * D ** -0.5),
        "ada_b": nrm((DEPTH, 6 * D), 0.01),
        "g_mix": gain((DEPTH, D)),
        "g_ffn": gain((DEPTH, D)),
        "ffn_w1": nrm((DEPTH, D, F), D ** -0.5),
        "ffn_w3": nrm((DEPTH, D, F), D ** -0.5),
        "ffn_w2": nrm((DEPTH, F, D), F ** -0.5),
        "a_w_in": nrm((N_A, D, 2 * E), D ** -0.5),
        "a_ln_g": gain((N_A, E)),
        "a_ln_b": nrm((N_A, E), 0.01),
        "a_w_s": nrm((N_A, GMLP_GROUPS, CHUNK, CHUNK), CHUNK ** -0.5),
        "a_b_s": gain((N_A, GMLP_GROUPS, CHUNK)),
        "a_w_out": nrm((N_A, E, D), E ** -0.5),
        "b_w_qkv": nrm((N_B, D, 3 * D), D ** -0.5),
        "b_rpb": nrm((N_B, NA_HEADS, 2 * NA_MAX_ROWS - 1, 2 * NA_COLS - 1), 0.1),
        "b_w_out": nrm((N_B, D, D), D ** -0.5),
        "c_w_pw1": nrm((N_C, D, 2 * D), D ** -0.5),
        "c_w_dw": nrm((N_C, CONV_WIDTH, D), CONV_WIDTH ** -0.5),
        "c_b_dw": nrm((N_C, D), 0.01),
        "c_ln_g": gain((N_C, D)),
        "c_ln_b": nrm((N_C, D), 0.01),
        "c_w_pw2": nrm((N_C, D, D), D ** -0.5),
        "g_final": gain((D,)),
    }


def reference(x, c, ctx, c_ctx, ada_w, ada_b, g_mix, g_ffn, ffn_w1, ffn_w3, ffn_w2,
              a_w_in, a_ln_g, a_ln_b, a_w_s, a_b_s, a_w_out,
              b_w_qkv, b_rpb, b_w_out,
              c_w_pw1, c_w_dw, c_b_dw, c_ln_g, c_ln_b, c_w_pw2, g_final):
    h_lat, h_ctx = x, ctx
    s_lat = jax.nn.silu(c)
    s_ctx = jax.nn.silu(c_ctx)
    for i in range(DEPTH):
        last = i == DEPTH - 1
        mixer = i % N_MIXERS
        j = i // N_MIXERS
        mod_l = (s_lat @ ada_w[i] + ada_b[i])[:, None, :]
        sh1, sc1, gt1, sh2, sc2, gt2 = jnp.split(mod_l, 6, axis=-1)
        ctx_needed = (not last) or mixer == 1
        if ctx_needed:
            mod_c = s_ctx @ ada_w[i] + ada_b[i]
            csh1, csc1, cgt1, csh2, csc2, cgt2 = jnp.split(mod_c, 6, axis=-1)
            hc = modulate(rmsnorm(h_ctx, g_mix[i]), csh1, csc1)
        hl = modulate(rmsnorm(h_lat, g_mix[i]), sh1, sc1)

        if mixer == 0:
            ml = gmlp_chunk_mixer(hl, a_w_in[j], a_ln_g[j], a_ln_b[j], a_w_s[j], a_b_s[j], a_w_out[j])
            mc = (gmlp_chunk_mixer(hc, a_w_in[j], a_ln_g[j], a_ln_b[j], a_w_s[j], a_b_s[j], a_w_out[j])
                  if not last else None)
        elif mixer == 1:
            ml, mc = neighbourhood_attention(hl, hc, b_w_qkv[j], b_rpb[j], b_w_out[j], not last)
        else:
            ml = conformer_conv(hl, c_w_pw1[j], c_w_dw[j], c_b_dw[j], c_ln_g[j], c_ln_b[j], c_w_pw2[j])
            mc = (conformer_conv(hc, c_w_pw1[j], c_w_dw[j], c_b_dw[j], c_ln_g[j], c_ln_b[j], c_w_pw2[j])
                  if not last else None)

        h_lat = h_lat + gt1 * ml
        hl = modulate(rmsnorm(h_lat, g_ffn[i]), sh2, sc2)
        h_lat = h_lat + gt2 * swiglu(hl, ffn_w1[i], ffn_w3[i], ffn_w2[i])
        if not last:
            h_ctx = h_ctx + cgt1 * mc
            hc = modulate(rmsnorm(h_ctx, g_ffn[i]), csh2, csc2)
            h_ctx = h_ctx + cgt2 * swiglu(hc, ffn_w1[i], ffn_w3[i], ffn_w2[i])
    return rmsnorm(h_lat, g_final)
```

```python
import functools

import jax
import jax.numpy as jnp
from jax import lax
from jax.experimental import pallas as pl
from jax.experimental.pallas import tpu as pltpu

GRID_W = 64
CHUNK = 128
GROUP_DIM = 128
HEAD_DIM = 128
NA_MAX_ROWS = 8
NA_COLS = 16
N_MIXERS = 3
EPS = 1e-6
NEG_INF = -1e30

NA_QR = 4
NA_KR = NA_QR + NA_MAX_ROWS
CONV_HALO = 16
CONV_RB = 64
LANES = 128
VMEM_LIMIT_BYTES = 56 * 1024 * 1024

F32 = jnp.float32
BF16 = jnp.bfloat16


def _params(*sem):
    return pltpu.CompilerParams(dimension_semantics=sem, vmem_limit_bytes=VMEM_LIMIT_BYTES)


def _tile(n, pref):
    t = min(n, pref)
    while n % t:
        t -= LANES
    return t


def _dot(a, b):
    return jnp.dot(a, b, preferred_element_type=F32)


def _dot_nt(a, b):
    return lax.dot_general(a, b, (((1,), (1,)), ((), ())), preferred_element_type=F32)


def _rms_modulate(x, g, shift, scale):
    y = x * lax.rsqrt(jnp.mean(x * x, axis=-1, keepdims=True) + EPS) * g
    return y * (1.0 + scale) + shift


def _layernorm(x, g, b):
    mu = jnp.mean(x, axis=-1, keepdims=True)
    xc = x - mu
    var = jnp.mean(xc * xc, axis=-1, keepdims=True)
    return xc * lax.rsqrt(var + EPS) * g + b


def _ada_kernel(s_ref, w_ref, b_ref, o_ref):
    s = s_ref[...]
    a = (s * jax.nn.sigmoid(s)).astype(BF16)
    o_ref[...] = _dot(a, w_ref[...].astype(BF16)) + b_ref[...]


def _ada_mods(c, c_ctx, ada_w, ada_b):
    depth, d, n = ada_w.shape
    bsz = c.shape[0]
    rows = 8 * pl.cdiv(bsz + 1, 8)
    s = jnp.zeros((rows, d), F32).at[:bsz].set(c).at[bsz].set(c_ctx)
    tn = _tile(n, 1024)
    out = pl.pallas_call(
        _ada_kernel,
        grid=(depth, n // tn),
        in_specs=[
            pl.BlockSpec((rows, d), lambda l, j: (0, 0)),
            pl.BlockSpec((None, d, tn), lambda l, j: (l, 0, j)),
            pl.BlockSpec((None, 1, tn), lambda l, j: (l, 0, j)),
        ],
        out_specs=pl.BlockSpec((None, rows, tn), lambda l, j: (l, 0, j)),
        out_shape=jax.ShapeDtypeStruct((depth, rows, n), F32),
        compiler_params=_params("arbitrary", "arbitrary"),
        name="ada_mod",
    )(s, ada_w, ada_b.reshape(depth, 1, n))
    return out[:, :bsz + 1].reshape(depth, bsz + 1, 6, d)


def _nm_matmul_kernel(x_ref, g_ref, mod_ref, *rest, mode, n_scaled, scale):
    if mode == "glu":
        wa_ref, wg_ref, o_ref, xn_ref = rest
    else:
        w_ref, o_ref, xn_ref = rest
    j = pl.program_id(1)

    @pl.when(j == 0)
    def _():
        xn_ref[...] = _rms_modulate(x_ref[...], g_ref[...], mod_ref[0:1, :], mod_ref[1:2, :]).astype(BF16)

    xn = xn_ref[...]
    if mode == "gelu":
        o = jax.nn.gelu(_dot(xn, w_ref[...]))
    elif mode == "glu":
        o = _dot(xn, wa_ref[...]) * jax.nn.sigmoid(_dot(xn, wg_ref[...]))
    else:
        o = _dot(xn, w_ref[...]) * jnp.where(j < n_scaled, scale, 1.0)
    o_ref[...] = o.astype(o_ref.dtype)


def _nm_matmul(x, g, mods, grp, w, *, mode, out_dtype, tm_pref=1024, tn_pref=1024):
    rows, d = x.shape
    n_out = w.shape[1] // 2 if mode == "glu" else w.shape[1]
    tm = grp.tile(rows, tm_pref)
    tn = _tile(d if mode == "qkv" else n_out, tn_pref)
    w_specs = [pl.BlockSpec((d, tn), lambda i, j: (0, j))]
    w_args = [w]
    if mode == "glu":
        off = n_out // tn
        w_specs.append(pl.BlockSpec((d, tn), lambda i, j: (0, j + off)))
        w_args.append(w)
    kern = functools.partial(_nm_matmul_kernel, mode=mode, n_scaled=d // tn, scale=HEAD_DIM ** -0.5)
    return pl.pallas_call(
        kern,
        grid=(rows // tm, n_out // tn),
        in_specs=[
            pl.BlockSpec((tm, d), lambda i, j: (i, 0)),
            pl.BlockSpec((1, d), lambda i, j: (0, 0)),
            pl.BlockSpec((None, 6, d), lambda i, j: (grp(i, tm), 0, 0)),
        ] + w_specs,
        out_specs=pl.BlockSpec((tm, tn), lambda i, j: (i, j)),
        out_shape=jax.ShapeDtypeStruct((rows, n_out), out_dtype),
        scratch_shapes=[pltpu.VMEM((tm, d), BF16)],
        compiler_params=_params("parallel", "arbitrary"),
        name="nm_matmul_" + mode,
    )(x, g.reshape(1, d), mods, *w_args)


class _Group:
    def __init__(self, seq_len, ctx_row=None):
        self.seq_len = seq_len
        self.ctx_row = ctx_row

    def tile(self, rows, pref):
        return _tile(rows if self.ctx_row is not None else self.seq_len, pref)

    def __call__(self, i, tm):
        if self.ctx_row is not None:
            return self.ctx_row
        return (i * tm) // self.seq_len


def _mm_residual_kernel(t_ref, w_ref, h_ref, mod_ref, o_ref):
    o_ref[...] = h_ref[...] + mod_ref[2:3, :] * _dot(t_ref[...], w_ref[...])


def _mm_residual(t, w, h, mods, grp, *, tm_pref=512):
    rows, d = h.shape
    k = t.shape[1]
    tm = grp.tile(rows, tm_pref)
    return pl.pallas_call(
        _mm_residual_kernel,
        grid=(rows // tm,),
        in_specs=[
            pl.BlockSpec((tm, k), lambda i: (i, 0)),
            pl.BlockSpec((k, d), lambda i: (0, 0)),
            pl.BlockSpec((tm, d), lambda i: (i, 0)),
            pl.BlockSpec((None, 6, d), lambda i: (grp(i, tm), 0, 0)),
        ],
        out_specs=pl.BlockSpec((tm, d), lambda i: (i, 0)),
        out_shape=jax.ShapeDtypeStruct((rows, d), F32),
        compiler_params=_params("parallel"),
        name="mm_residual",
    )(t, w, h, mods)


def _ffn_kernel(x_ref, g_ref, mod_ref, w1_ref, w3_ref, w2_ref, gf_ref, o_ref, xn_ref, acc_ref, *, final_norm):
    k = pl.program_id(1)

    @pl.when(k == 0)
    def _():
        xn_ref[...] = _rms_modulate(x_ref[...], g_ref[...], mod_ref[3:4, :], mod_ref[4:5, :]).astype(BF16)
        acc_ref[...] = jnp.zeros_like(acc_ref)

    xn = xn_ref[...]
    a = _dot(xn, w1_ref[...])
    b = _dot(xn, w3_ref[...])
    acc_ref[...] += _dot((a * jax.nn.sigmoid(a) * b).astype(BF16), w2_ref[...])

    @pl.when(k == pl.num_programs(1) - 1)
    def _():
        y = x_ref[...] + mod_ref[5:6, :] * acc_ref[...]
        if final_norm:
            y = y * lax.rsqrt(jnp.mean(y * y, axis=-1, keepdims=True) + EPS) * gf_ref[...]
        o_ref[...] = y


def _ffn(x, g, mods, grp, w1, w3, w2, g_final, *, final_norm, tm_pref=512, tf_pref=512):
    rows, d = x.shape
    f = w1.shape[1]
    tm = grp.tile(rows, tm_pref)
    tf = _tile(f, tf_pref)
    return pl.pallas_call(
        functools.partial(_ffn_kernel, final_norm=final_norm),
        grid=(rows // tm, f // tf),
        in_specs=[
            pl.BlockSpec((tm, d), lambda i, k: (i, 0)),
            pl.BlockSpec((1, d), lambda i, k: (0, 0)),
            pl.BlockSpec((None, 6, d), lambda i, k: (grp(i, tm), 0, 0)),
            pl.BlockSpec((d, tf), lambda i, k: (0, k)),
            pl.BlockSpec((d, tf), lambda i, k: (0, k)),
            pl.BlockSpec((tf, d), lambda i, k: (k, 0)),
            pl.BlockSpec((1, d), lambda i, k: (0, 0)),
        ],
        out_specs=pl.BlockSpec((tm, d), lambda i, k: (i, 0)),
        out_shape=jax.ShapeDtypeStruct((rows, d), F32),
        scratch_shapes=[pltpu.VMEM((tm, d), BF16), pltpu.VMEM((tm, d), F32)],
        compiler_params=_params("parallel", "arbitrary"),
        name="ffn",
    )(x, g.reshape(1, d), mods, w1, w3, w2, g_final.reshape(1, d))


def _gmlp_kernel(u_ref, v_ref, lng_ref, lnb_ref, ws_ref, bs_ref, wout_ref, h_ref, mod_ref, o_ref,
                 vn_ref, t_ref, *, tm, groups):
    vn_ref[...] = _layernorm(v_ref[...], lng_ref[...], lnb_ref[...]).astype(BF16)
    for c in range(tm // CHUNK):
        r = pl.ds(c * CHUNK, CHUNK)
        for g in range(groups):
            cols = pl.ds(g * GROUP_DIM, GROUP_DIM)
            sv = _dot(ws_ref[g], vn_ref[r, cols]) + bs_ref[:, cols]
            t_ref[r, cols] = (u_ref[r, cols] * sv).astype(BF16)
    o_ref[...] = h_ref[...] + mod_ref[2:3, :] * _dot(t_ref[...], wout_ref[...])


def _gmlp_gate_out(z, ln_g, ln_b, w_s, b_full, w_out, h, mods, grp, *, tm_pref=256):
    rows, d = h.shape
    e = z.shape[1] // 2
    groups = e // GROUP_DIM
    tm = grp.tile(rows, tm_pref)
    return pl.pallas_call(
        functools.partial(_gmlp_kernel, tm=tm, groups=groups),
        grid=(rows // tm,),
        in_specs=[
            pl.BlockSpec((tm, e), lambda i: (i, 0)),
            pl.BlockSpec((tm, e), lambda i: (i, 1)),
            pl.BlockSpec((1, e), lambda i: (0, 0)),
            pl.BlockSpec((1, e), lambda i: (0, 0)),
            pl.BlockSpec((groups, CHUNK, CHUNK), lambda i: (0, 0, 0)),
            pl.BlockSpec((CHUNK, e), lambda i: (0, 0)),
            pl.BlockSpec((e, d), lambda i: (0, 0)),
            pl.BlockSpec((tm, d), lambda i: (i, 0)),
            pl.BlockSpec((None, 6, d), lambda i: (grp(i, tm), 0, 0)),
        ],
        out_specs=pl.BlockSpec((tm, d), lambda i: (i, 0)),
        out_shape=jax.ShapeDtypeStruct((rows, d), F32),
        scratch_shapes=[pltpu.VMEM((tm, e), BF16), pltpu.VMEM((tm, e), BF16)],
        compiler_params=_params("parallel"),
        name="gmlp_gate_out",
    )(z, z, ln_g.reshape(1, e), ln_b.reshape(1, e), w_s, b_full, w_out, h, mods)


def _na_bias_tables(rpb, rows):
    kr = min(NA_MAX_ROWS, rows)
    nblk = rows // NA_QR
    cols = jnp.arange(GRID_W)
    c_start = jnp.clip(cols - NA_COLS // 2, 0, GRID_W - NA_COLS)
    col_ok = (cols[None, :] >= c_start[:, None]) & (cols[None, :] < c_start[:, None] + NA_COLS)
    dc_idx = jnp.clip(cols[None, :] - cols[:, None] + NA_COLS - 1, 0, 2 * NA_COLS - 2)
    tabs = []
    for m in (0, 1, nblk - 1):
        base = min(max(NA_QR * m - kr // 2, 0), rows - NA_KR)
        r = NA_QR * m + jnp.arange(NA_QR)
        r_start = jnp.clip(r - kr // 2, 0, rows - kr)
        kabs = base + jnp.arange(NA_KR)
        row_ok = (kabs[None, :] >= r_start[:, None]) & (kabs[None, :] < r_start[:, None] + kr)
        dr_idx = jnp.clip(kabs[None, :] - r[:, None] + NA_MAX_ROWS - 1, 0, 2 * NA_MAX_ROWS - 2)
        bias = rpb[:, dr_idx][:, :, :, dc_idx]
        ok = row_ok[:, :, None, None] & col_ok[None, None, :, :]
        bias = jnp.where(ok[None], bias.astype(F32), NEG_INF).transpose(0, 1, 3, 2, 4)
        tabs.append(bias.reshape(rpb.shape[0], NA_QR * GRID_W, NA_KR * GRID_W))
    return jnp.stack(tabs)


def _na_kernel(q_ref, k_ref, v_ref, kc_ref, vc_ref, bias_ref, o_ref, *, rows):
    m = pl.program_id(2)
    kr = min(NA_MAX_ROWS, rows)
    base = jnp.clip(NA_QR * m - kr // 2, 0, rows - NA_KR) * GRID_W
    base = pl.multiple_of(base, NA_QR * GRID_W)
    win = pl.ds(base, NA_KR * GRID_W)
    q = q_ref[...]
    s1 = _dot_nt(q, k_ref[win, :]) + bias_ref[...]
    s2 = _dot_nt(q, kc_ref[...])
    mx = jnp.maximum(jnp.max(s1, axis=-1, keepdims=True), jnp.max(s2, axis=-1, keepdims=True))
    p1 = jnp.exp(s1 - mx)
    p2 = jnp.exp(s2 - mx)
    den = jnp.sum(p1, axis=-1, keepdims=True) + jnp.sum(p2, axis=-1, keepdims=True)
    o = _dot(p1.astype(BF16), v_ref[win, :]) + _dot(p2.astype(BF16), vc_ref[...])
    o_ref[...] = (o / den).astype(o_ref.dtype)


def _na_lat(qkv, qkv_ctx, bias, bsz, seq_len, ctx_len):
    d = qkv.shape[1] // 3
    heads = d // HEAD_DIM
    rows = seq_len // GRID_W
    nblk = rows // NA_QR
    tq = NA_QR * GRID_W
    return pl.pallas_call(
        functools.partial(_na_kernel, rows=rows),
        grid=(bsz, heads, nblk),
        in_specs=[
            pl.BlockSpec((tq, HEAD_DIM), lambda b, h, m: (b * nblk + m, h)),
            pl.BlockSpec((seq_len, HEAD_DIM), lambda b, h, m: (b, heads + h)),
            pl.BlockSpec((seq_len, HEAD_DIM), lambda b, h, m: (b, 2 * heads + h)),
            pl.BlockSpec((ctx_len, HEAD_DIM), lambda b, h, m: (b, heads + h)),
            pl.BlockSpec((ctx_len, HEAD_DIM), lambda b, h, m: (b, 2 * heads + h)),
            pl.BlockSpec((None, None, tq, NA_KR * GRID_W),
                         lambda b, h, m: (jnp.where(m == 0, 0, jnp.where(m == nblk - 1, 2, 1)), h, 0, 0)),
        ],
        out_specs=pl.BlockSpec((tq, HEAD_DIM), lambda b, h, m: (b * nblk + m, h)),
        out_shape=jax.ShapeDtypeStruct((bsz * seq_len, d), BF16),
        compiler_params=_params("parallel", "parallel", "arbitrary"),
        name="na_latent",
    )(qkv, qkv, qkv, qkv_ctx, qkv_ctx, bias)


def _ctx_attn_kernel(q_ref, k_ref, v_ref, o_ref):
    s = _dot_nt(q_ref[...], k_ref[...])
    p = jnp.exp(s - jnp.max(s, axis=-1, keepdims=True))
    den = jnp.sum(p, axis=-1, keepdims=True)
    o_ref[...] = (_dot(p.astype(BF16), v_ref[...]) / den).astype(o_ref.dtype)


def _na_ctx(qkv_ctx, bsz, ctx_len):
    d = qkv_ctx.shape[1] // 3
    heads = d // HEAD_DIM
    return pl.pallas_call(
        _ctx_attn_kernel,
        grid=(bsz, heads),
        in_specs=[
            pl.BlockSpec((ctx_len, HEAD_DIM), lambda b, h: (b, h)),
            pl.BlockSpec((ctx_len, HEAD_DIM), lambda b, h: (b, heads + h)),
            pl.BlockSpec((ctx_len, HEAD_DIM), lambda b, h: (b, 2 * heads + h)),
        ],
        out_specs=pl.BlockSpec((ctx_len, HEAD_DIM), lambda b, h: (b, h)),
        out_shape=jax.ShapeDtypeStruct((bsz * ctx_len, d), BF16),
        compiler_params=_params("parallel", "parallel"),
        name="na_context",
    )(qkv_ctx, qkv_ctx, qkv_ctx)


def _conv_kernel(yp_ref, ym_ref, yn_ref, wdw_ref, bdw_ref, lng_ref, lnb_ref, w2_ref, h_ref, mod_ref, o_ref,
                 win_ref, c_ref, *, tm, tiles_per_seq, width):
    i = pl.program_id(0)
    first = (i % tiles_per_seq) == 0
    last = (i % tiles_per_seq) == tiles_per_seq - 1
    win_ref[0:CONV_HALO, :] = jnp.where(first, 0.0, yp_ref[...])
    win_ref[CONV_HALO:CONV_HALO + tm, :] = ym_ref[...]
    win_ref[CONV_HALO + tm:, :] = jnp.where(last, 0.0, yn_ref[...])
    lead = CONV_HALO - width // 2

    def col_block(cc, carry):
        cols = pl.ds(pl.multiple_of(cc * LANES, LANES), LANES)
        for rb in range(tm // CONV_RB):
            r0 = rb * CONV_RB
            acc = jnp.broadcast_to(bdw_ref[:, cols], (CONV_RB, LANES))
            for k in range(width):
                acc = acc + wdw_ref[k:k + 1, cols] * win_ref[r0 + lead + k:r0 + lead + k + CONV_RB, cols]
            c_ref[r0:r0 + CONV_RB, cols] = acc
        return carry

    lax.fori_loop(0, c_ref.shape[1] // LANES, col_block, 0)
    y = _layernorm(c_ref[...], lng_ref[...], lnb_ref[...])
    y = (y * jax.nn.sigmoid(y)).astype(BF16)
    o_ref[...] = h_ref[...] + mod_ref[2:3, :] * _dot(y, w2_ref[...])


def _conv_out(y, w_dw, b_dw, ln_g, ln_b, w_pw2, h, mods, grp, seq_len, *, tm_pref=256):
    rows, d = h.shape
    width = w_dw.shape[0]
    tm = _tile(seq_len, tm_pref)
    hb = tm // CONV_HALO
    n_halo = rows // CONV_HALO
    wpad = jnp.zeros((8 * pl.cdiv(width, 8), d), F32).at[:width].set(w_dw)
    kern = functools.partial(_conv_kernel, tm=tm, tiles_per_seq=seq_len // tm, width=width)
    return pl.pallas_call(
        kern,
        grid=(rows // tm,),
        in_specs=[
            pl.BlockSpec((CONV_HALO, d), lambda i: (jnp.maximum(i * hb - 1, 0), 0)),
            pl.BlockSpec((tm, d), lambda i: (i, 0)),
            pl.BlockSpec((CONV_HALO, d), lambda i: (jnp.minimum((i + 1) * hb, n_halo - 1), 0)),
            pl.BlockSpec(wpad.shape, lambda i: (0, 0)),
            pl.BlockSpec((1, d), lambda i: (0, 0)),
            pl.BlockSpec((1, d), lambda i: (0, 0)),
            pl.BlockSpec((1, d), lambda i: (0, 0)),
            pl.BlockSpec((d, d), lambda i: (0, 0)),
            pl.BlockSpec((tm, d), lambda i: (i, 0)),
            pl.BlockSpec((None, 6, d), lambda i: (grp(i, tm), 0, 0)),
        ],
        out_specs=pl.BlockSpec((tm, d), lambda i: (i, 0)),
        out_shape=jax.ShapeDtypeStruct((rows, d), F32),
        scratch_shapes=[pltpu.VMEM((tm + 2 * CONV_HALO, d), F32), pltpu.VMEM((tm, d), F32)],
        compiler_params=_params("parallel"),
        name="conv_out",
    )(y, y, y, wpad, b_dw.reshape(1, d), ln_g.reshape(1, d), ln_b.reshape(1, d), w_pw2, h, mods)


def kernel(x, c, ctx, c_ctx, ada_w, ada_b, g_mix, g_ffn, ffn_w1, ffn_w3, ffn_w2,
           a_w_in, a_ln_g, a_ln_b, a_w_s, a_b_s, a_w_out,
           b_w_qkv, b_rpb, b_w_out,
           c_w_pw1, c_w_dw, c_b_dw, c_ln_g, c_ln_b, c_w_pw2, g_final):
    bsz, seq_len, d = x.shape
    ctx_len = ctx.shape[1]
    depth = ada_w.shape[0]
    assert seq_len % (NA_QR * GRID_W) == 0 and seq_len // GRID_W >= NA_KR
    assert seq_len % CHUNK == 0 and ctx_len % CHUNK == 0 and d % LANES == 0
    assert c_w_dw.shape[1] // 2 <= CONV_HALO

    bf = lambda w: w.astype(BF16)
    ffn_w1, ffn_w3, ffn_w2 = bf(ffn_w1), bf(ffn_w3), bf(ffn_w2)
    a_w_in, a_w_s, a_w_out = bf(a_w_in), bf(a_w_s), bf(a_w_out)
    b_w_qkv, b_w_out = bf(b_w_qkv), bf(b_w_out)
    c_w_pw1, c_w_pw2 = bf(c_w_pw1), bf(c_w_pw2)

    mods = _ada_mods(c, c_ctx, ada_w, ada_b)
    lat_grp = _Group(seq_len)
    ctx_grp = _Group(ctx_len, ctx_row=bsz)
    streams = [(lat_grp, seq_len), (ctx_grp, ctx_len)]

    h = [x.reshape(bsz * seq_len, d), ctx.reshape(bsz * ctx_len, d)]
    for i in range(depth):
        last = i == depth - 1
        mixer, j = i % N_MIXERS, i // N_MIXERS
        m = mods[i]
        ctx_live = any(l % N_MIXERS == 1 for l in range(i + 1, depth))
        active = [0, 1] if ctx_live else [0]

        if mixer == 0:
            e = a_w_in.shape[2] // 2
            b_full = jnp.broadcast_to(a_b_s[j].T[:, :, None], (CHUNK, e // GROUP_DIM, GROUP_DIM)).reshape(CHUNK, e)
            for s in active:
                grp, _ = streams[s]
                z = _nm_matmul(h[s], g_mix[i], m, grp, a_w_in[j], mode="gelu", out_dtype=F32)
                h[s] = _gmlp_gate_out(z, a_ln_g[j], a_ln_b[j], a_w_s[j], b_full, a_w_out[j], h[s], m, grp)
        elif mixer == 1:
            qkv = [_nm_matmul(h[s], g_mix[i], m, streams[s][0], b_w_qkv[j], mode="qkv", out_dtype=BF16)
                   for s in (0, 1)]
            bias = _na_bias_tables(b_rpb[j], seq_len // GRID_W)
            o_lat = _na_lat(qkv[0], qkv[1], bias, bsz, seq_len, ctx_len)
            h[0] = _mm_residual(o_lat, b_w_out[j], h[0], m, lat_grp)
            if ctx_live:
                o_ctx = _na_ctx(qkv[1], bsz, ctx_len)
                h[1] = _mm_residual(o_ctx, b_w_out[j], h[1], m, ctx_grp)
        else:
            for s in active:
                grp, slen = streams[s]
                y = _nm_matmul(h[s], g_mix[i], m, grp, c_w_pw1[j], mode="glu", out_dtype=F32, tn_pref=512)
                h[s] = _conv_out(y, c_w_dw[j], c_b_dw[j], c_ln_g[j], c_ln_b[j], c_w_pw2[j], h[s], m, grp, slen)

        for s in active:
            grp, _ = streams[s]
            h[s] = _ffn(h[s], g_ffn[i], m, grp, ffn_w1[i], ffn_w3[i], ffn_w2[i], g_final,
                        final_norm=last and s == 0)
    return h[0].reshape(bsz, seq_len, d)
```

```python
import functools

import jax
import jax.numpy as jnp
from jax import lax
from jax.experimental import pallas as pl
from jax.experimental.pallas import tpu as pltpu

GRID_W = 64
CHUNK = 128
GROUP_DIM = 128
HEAD_DIM = 128
NA_MAX_ROWS = 8
NA_COLS = 16
N_MIXERS = 3
EPS = 1e-6
NEG_INF = -1e30

NA_QR = 4
NA_KR = NA_QR + NA_MAX_ROWS
CONV_HALO = 16
CONV_RB = 64
NORM_CHUNK_ROWS = 256
LANES = 128
SUBLANES = 8
VMEM_LIMIT_BYTES = 56 * 1024 * 1024

F32 = jnp.float32
BF16 = jnp.bfloat16


def _params(*sem):
    return pltpu.CompilerParams(dimension_semantics=sem, vmem_limit_bytes=VMEM_LIMIT_BYTES)


def _tile(n, pref):
    t = min(n, pref)
    while n % t:
        t -= LANES
    return t


def _dot(a, b):
    return jnp.dot(a, b, preferred_element_type=F32)


def _dot_nt(a, b):
    return lax.dot_general(a, b, (((1,), (1,)), ((), ())), preferred_element_type=F32)


def _rms_modulate(x, g, shift, scale):
    gain = g * (1.0 + scale)
    return (x * lax.rsqrt(jnp.mean(x * x, axis=-1, keepdims=True) + EPS)) * gain + shift


def _layernorm(x, g, b):
    mu = jnp.mean(x, axis=-1, keepdims=True)
    xc = x - mu
    var = jnp.mean(xc * xc, axis=-1, keepdims=True)
    return xc * lax.rsqrt(var + EPS) * g + b


def _ada_kernel(s_ref, w_ref, b_ref, o_ref):
    s = s_ref[...]
    a = (s * jax.nn.sigmoid(s)).astype(BF16)
    o_ref[...] = _dot(a, w_ref[...].astype(BF16)) + b_ref[...]


def _ada_mods(c, c_ctx, ada_w, ada_b):
    depth, d, n = ada_w.shape
    bsz = c.shape[0]
    rows = 8 * pl.cdiv(bsz + 1, 8)
    s = jnp.zeros((rows, d), F32).at[:bsz].set(c).at[bsz].set(c_ctx)
    tn = _tile(n, 1024)
    out = pl.pallas_call(
        _ada_kernel,
        grid=(depth, n // tn),
        in_specs=[
            pl.BlockSpec((rows, d), lambda l, j: (0, 0)),
            pl.BlockSpec((None, d, tn), lambda l, j: (l, 0, j)),
            pl.BlockSpec((None, 1, tn), lambda l, j: (l, 0, j)),
        ],
        out_specs=pl.BlockSpec((None, rows, tn), lambda l, j: (l, 0, j)),
        out_shape=jax.ShapeDtypeStruct((depth, rows, n), F32),
        compiler_params=_params("arbitrary", "arbitrary"),
        name="ada_mod",
    )(s, ada_w, ada_b.reshape(depth, 1, n))
    return out[:, :bsz + 1].reshape(depth, bsz + 1, 6, d)


def _nm_matmul_kernel(x_ref, g_ref, mod_ref, *rest, mode, n_scaled, scale):
    if mode == "glu":
        wa_ref, wg_ref, o_ref, xn_ref = rest
    else:
        w_ref, o_ref, xn_ref = rest
    j = pl.program_id(1)

    def project(xn):
        if mode == "gelu":
            o = jax.nn.gelu(_dot(xn, w_ref[...]))
        elif mode == "glu":
            o = _dot(xn, wa_ref[...]) * jax.nn.sigmoid(_dot(xn, wg_ref[...]))
        else:
            o = _dot(xn, w_ref[...]) * jnp.where(j < n_scaled, scale, 1.0)
        return o.astype(o_ref.dtype)

    @pl.when(j == 0)
    def _():
        tm = x_ref.shape[0]
        cr = _tile(tm, NORM_CHUNK_ROWS)
        for c in range(tm // cr):
            r = pl.ds(c * cr, cr)
            xn = _rms_modulate(x_ref[r, :], g_ref[...], mod_ref[0:1, :], mod_ref[1:2, :]).astype(BF16)
            xn_ref[r, :] = xn
            o_ref[r, :] = project(xn)

    @pl.when(j > 0)
    def _():
        o_ref[...] = project(xn_ref[...])


def _nm_matmul(x, g, mods, grp, w, layer, *, mode, out_dtype, tm_pref=1024, tn_pref=1024):
    rows, d = x.shape
    n_out = w.shape[2] // 2 if mode == "glu" else w.shape[2]
    tm = grp.tile(rows, tm_pref)
    tn = _tile(d if mode == "qkv" else n_out, tn_pref)
    w_specs = [pl.BlockSpec((None, d, tn), lambda i, j: (layer, 0, j))]
    w_args = [w]
    if mode == "glu":
        off = n_out // tn
        w_specs.append(pl.BlockSpec((None, d, tn), lambda i, j: (layer, 0, j + off)))
        w_args.append(w)
    kern = functools.partial(_nm_matmul_kernel, mode=mode, n_scaled=d // tn, scale=HEAD_DIM ** -0.5)
    return pl.pallas_call(
        kern,
        grid=(rows // tm, n_out // tn),
        in_specs=[
            pl.BlockSpec((tm, d), lambda i, j: (i, 0)),
            pl.BlockSpec((1, d), lambda i, j: (0, 0)),
            pl.BlockSpec((None, 6, d), lambda i, j: (grp(i, tm), 0, 0)),
        ] + w_specs,
        out_specs=pl.BlockSpec((tm, tn), lambda i, j: (i, j)),
        out_shape=jax.ShapeDtypeStruct((rows, n_out), out_dtype),
        scratch_shapes=[pltpu.VMEM((tm, d), BF16)],
        compiler_params=_params("parallel", "arbitrary"),
        name="nm_matmul_" + mode,
    )(x, g.reshape(1, d), mods, *w_args)


class _Group:
    def __init__(self, seq_len, ctx_row=None):
        self.seq_len = seq_len
        self.ctx_row = ctx_row

    def tile(self, rows, pref):
        return _tile(rows if self.ctx_row is not None else self.seq_len, pref)

    def __call__(self, i, tm):
        if self.ctx_row is not None:
            return self.ctx_row
        return (i * tm) // self.seq_len


def _mm_residual_kernel(t_ref, w_ref, h_ref, mod_ref, o_ref):
    o_ref[...] = h_ref[...] + mod_ref[2:3, :] * _dot(t_ref[...], w_ref[...])


def _mm_residual(t, w, layer, h, mods, grp, *, tm_pref=512):
    rows, d = h.shape
    k = t.shape[1]
    tm = grp.tile(rows, tm_pref)
    return pl.pallas_call(
        _mm_residual_kernel,
        grid=(rows // tm,),
        in_specs=[
            pl.BlockSpec((tm, k), lambda i: (i, 0)),
            pl.BlockSpec((None, k, d), lambda i: (layer, 0, 0)),
            pl.BlockSpec((tm, d), lambda i: (i, 0)),
            pl.BlockSpec((None, 6, d), lambda i: (grp(i, tm), 0, 0)),
        ],
        out_specs=pl.BlockSpec((tm, d), lambda i: (i, 0)),
        out_shape=jax.ShapeDtypeStruct((rows, d), F32),
        compiler_params=_params("parallel"),
        name="mm_residual",
    )(t, w, h, mods)


def _ffn_kernel(x_ref, g_ref, mod_ref, w1_ref, w3_ref, w2_ref, gf_ref, o_ref, xn_ref, acc_ref, *, final_norm):
    k = pl.program_id(1)

    def partial_out(xn):
        a = _dot(xn, w1_ref[...])
        b = _dot(xn, w3_ref[...])
        return _dot((a * jax.nn.sigmoid(a) * b).astype(BF16), w2_ref[...])

    @pl.when(k == 0)
    def _():
        tm = x_ref.shape[0]
        cr = _tile(tm, NORM_CHUNK_ROWS)
        for c in range(tm // cr):
            r = pl.ds(c * cr, cr)
            xn = _rms_modulate(x_ref[r, :], g_ref[...], mod_ref[3:4, :], mod_ref[4:5, :]).astype(BF16)
            xn_ref[r, :] = xn
            acc_ref[r, :] = partial_out(xn)

    @pl.when(k > 0)
    def _():
        acc_ref[...] += partial_out(xn_ref[...])

    @pl.when(k == pl.num_programs(1) - 1)
    def _():
        y = x_ref[...] + mod_ref[5:6, :] * acc_ref[...]
        if final_norm:
            y = y * lax.rsqrt(jnp.mean(y * y, axis=-1, keepdims=True) + EPS) * gf_ref[...]
        o_ref[...] = y


def _ffn(x, g, mods, grp, w1, w3, w2, layer, g_final, *, final_norm, tm_pref=512, tf_pref=512):
    rows, d = x.shape
    f = w1.shape[2]
    tm = grp.tile(rows, tm_pref)
    tf = _tile(f, tf_pref)
    return pl.pallas_call(
        functools.partial(_ffn_kernel, final_norm=final_norm),
        grid=(rows // tm, f // tf),
        in_specs=[
            pl.BlockSpec((tm, d), lambda i, k: (i, 0)),
            pl.BlockSpec((1, d), lambda i, k: (0, 0)),
            pl.BlockSpec((None, 6, d), lambda i, k: (grp(i, tm), 0, 0)),
            pl.BlockSpec((None, d, tf), lambda i, k: (layer, 0, k)),
            pl.BlockSpec((None, d, tf), lambda i, k: (layer, 0, k)),
            pl.BlockSpec((None, tf, d), lambda i, k: (layer, k, 0)),
            pl.BlockSpec((1, d), lambda i, k: (0, 0)),
        ],
        out_specs=pl.BlockSpec((tm, d), lambda i, k: (i, 0)),
        out_shape=jax.ShapeDtypeStruct((rows, d), F32),
        scratch_shapes=[pltpu.VMEM((tm, d), BF16), pltpu.VMEM((tm, d), F32)],
        compiler_params=_params("parallel", "arbitrary"),
        name="ffn",
    )(x, g.reshape(1, d), mods, w1, w3, w2, g_final.reshape(1, d))


def _gmlp_kernel(u_ref, v_ref, lng_ref, lnb_ref, ws_ref, bs_ref, wout_ref, h_ref, mod_ref, o_ref,
                 vn_ref, t_ref, *, tm, groups):
    vn_ref[...] = _layernorm(v_ref[...], lng_ref[...], lnb_ref[...]).astype(BF16)
    for c in range(tm // CHUNK):
        r = pl.ds(c * CHUNK, CHUNK)
        for g in range(groups):
            cols = pl.ds(g * GROUP_DIM, GROUP_DIM)
            sv = _dot(ws_ref[g], vn_ref[r, cols]) + bs_ref[:, cols]
            t_ref[r, cols] = (u_ref[r, cols] * sv).astype(BF16)
    o_ref[...] = h_ref[...] + mod_ref[2:3, :] * _dot(t_ref[...], wout_ref[...])


def _gmlp_gate_out(z, ln_g, ln_b, w_s, b_full, w_out, layer, h, mods, grp, *, tm_pref=256):
    rows, d = h.shape
    e = z.shape[1] // 2
    groups = e // GROUP_DIM
    tm = grp.tile(rows, tm_pref)
    return pl.pallas_call(
        functools.partial(_gmlp_kernel, tm=tm, groups=groups),
        grid=(rows // tm,),
        in_specs=[
            pl.BlockSpec((tm, e), lambda i: (i, 0)),
            pl.BlockSpec((tm, e), lambda i: (i, 1)),
            pl.BlockSpec((1, e), lambda i: (0, 0)),
            pl.BlockSpec((1, e), lambda i: (0, 0)),
            pl.BlockSpec((None, groups, CHUNK, CHUNK), lambda i: (layer, 0, 0, 0)),
            pl.BlockSpec((CHUNK, e), lambda i: (0, 0)),
            pl.BlockSpec((None, e, d), lambda i: (layer, 0, 0)),
            pl.BlockSpec((tm, d), lambda i: (i, 0)),
            pl.BlockSpec((None, 6, d), lambda i: (grp(i, tm), 0, 0)),
        ],
        out_specs=pl.BlockSpec((tm, d), lambda i: (i, 0)),
        out_shape=jax.ShapeDtypeStruct((rows, d), F32),
        scratch_shapes=[pltpu.VMEM((tm, e), BF16), pltpu.VMEM((tm, e), BF16)],
        compiler_params=_params("parallel"),
        name="gmlp_gate_out",
    )(z, z, ln_g.reshape(1, e), ln_b.reshape(1, e), w_s, b_full, w_out, h, mods)


def _na_bias_tables(rpb, rows):
    kr = min(NA_MAX_ROWS, rows)
    nblk = rows // NA_QR
    cols = jnp.arange(GRID_W)
    c_start = jnp.clip(cols - NA_COLS // 2, 0, GRID_W - NA_COLS)
    col_ok = (cols[None, :] >= c_start[:, None]) & (cols[None, :] < c_start[:, None] + NA_COLS)
    dc_idx = jnp.clip(cols[None, :] - cols[:, None] + NA_COLS - 1, 0, 2 * NA_COLS - 2)
    slabs = jnp.where(col_ok[None, None], rpb[:, :, dc_idx].astype(F32), NEG_INF)
    masked = jnp.full((rpb.shape[0], GRID_W, GRID_W), NEG_INF, F32)
    tabs = []
    for m in (0, 1, nblk - 1):
        base = min(max(NA_QR * m - kr // 2, 0), rows - NA_KR)
        q_rows = []
        for ri in range(NA_QR):
            r = NA_QR * m + ri
            r_start = min(max(r - kr // 2, 0), rows - kr)
            keys = [slabs[:, base + ki - r + NA_MAX_ROWS - 1] if r_start <= base + ki < r_start + kr else masked
                    for ki in range(NA_KR)]
            q_rows.append(jnp.concatenate(keys, axis=-1))
        tabs.append(jnp.concatenate(q_rows, axis=1))
    return jnp.stack(tabs)


def _na_kernel(q_ref, k_ref, v_ref, kc_ref, vc_ref, bias_ref, o_ref, *, rows, heads_per_step):
    kr = min(NA_MAX_ROWS, rows)
    nblk = rows // NA_QR
    tq = NA_QR * GRID_W

    def q_block(m, carry):
        base = jnp.clip(NA_QR * m - kr // 2, 0, rows - NA_KR) * GRID_W
        win = pl.ds(pl.multiple_of(base, tq), NA_KR * GRID_W)
        qrows = pl.ds(pl.multiple_of(m * tq, tq), tq)
        variant = jnp.where(m == 0, 0, jnp.where(m == nblk - 1, 2, 1))
        for hh in range(heads_per_step):
            cols = pl.ds(hh * HEAD_DIM, HEAD_DIM)
            q = q_ref[qrows, cols]
            s1 = _dot_nt(q, k_ref[win, cols]) + bias_ref[variant, hh]
            s2 = _dot_nt(q, kc_ref[:, cols])
            mx = jnp.maximum(jnp.max(s1, axis=-1, keepdims=True), jnp.max(s2, axis=-1, keepdims=True))
            p1 = jnp.exp(s1 - mx)
            p2 = jnp.exp(s2 - mx)
            den = jnp.sum(p1, axis=-1, keepdims=True) + jnp.sum(p2, axis=-1, keepdims=True)
            o = _dot(p1.astype(BF16), v_ref[win, cols]) + _dot(p2.astype(BF16), vc_ref[:, cols])
            o_ref[qrows, cols] = (o / den).astype(o_ref.dtype)
        return carry

    lax.fori_loop(0, nblk, q_block, 0)


def _na_lat(qkv, qkv_ctx, bias, bsz, seq_len, ctx_len, *, heads_per_step=2):
    d = qkv.shape[1] // 3
    heads = d // HEAD_DIM
    rows = seq_len // GRID_W
    hp = heads_per_step
    groups = heads // hp
    wide = hp * HEAD_DIM
    return pl.pallas_call(
        functools.partial(_na_kernel, rows=rows, heads_per_step=hp),
        grid=(bsz, groups),
        in_specs=[
            pl.BlockSpec((seq_len, wide), lambda b, g: (b, g)),
            pl.BlockSpec((seq_len, wide), lambda b, g: (b, groups + g)),
            pl.BlockSpec((seq_len, wide), lambda b, g: (b, 2 * groups + g)),
            pl.BlockSpec((ctx_len, wide), lambda b, g: (b, groups + g)),
            pl.BlockSpec((ctx_len, wide), lambda b, g: (b, 2 * groups + g)),
            pl.BlockSpec((3, hp, NA_QR * GRID_W, NA_KR * GRID_W), lambda b, g: (0, g, 0, 0)),
        ],
        out_specs=pl.BlockSpec((seq_len, wide), lambda b, g: (b, g)),
        out_shape=jax.ShapeDtypeStruct((bsz * seq_len, d), BF16),
        compiler_params=_params("parallel", "parallel"),
        name="na_latent",
    )(qkv, qkv, qkv, qkv_ctx, qkv_ctx, bias)


def _ctx_attn_kernel(q_ref, k_ref, v_ref, o_ref):
    s = _dot_nt(q_ref[...], k_ref[...])
    p = jnp.exp(s - jnp.max(s, axis=-1, keepdims=True))
    den = jnp.sum(p, axis=-1, keepdims=True)
    o_ref[...] = (_dot(p.astype(BF16), v_ref[...]) / den).astype(o_ref.dtype)


def _na_ctx(qkv_ctx, bsz, ctx_len):
    d = qkv_ctx.shape[1] // 3
    heads = d // HEAD_DIM
    return pl.pallas_call(
        _ctx_attn_kernel,
        grid=(bsz, heads),
        in_specs=[
            pl.BlockSpec((ctx_len, HEAD_DIM), lambda b, h: (b, h)),
            pl.BlockSpec((ctx_len, HEAD_DIM), lambda b, h: (b, heads + h)),
            pl.BlockSpec((ctx_len, HEAD_DIM), lambda b, h: (b, 2 * heads + h)),
        ],
        out_specs=pl.BlockSpec((ctx_len, HEAD_DIM), lambda b, h: (b, h)),
        out_shape=jax.ShapeDtypeStruct((bsz * ctx_len, d), BF16),
        compiler_params=_params("parallel", "parallel"),
        name="na_context",
    )(qkv_ctx, qkv_ctx, qkv_ctx)


def _conv_kernel(yp_ref, ym_ref, yn_ref, wdw_ref, bdw_ref, lng_ref, lnb_ref, w2_ref, h_ref, mod_ref, o_ref,
                 win_ref, c_ref, *, tm, tiles_per_seq, width):
    i = pl.program_id(0)
    first = (i % tiles_per_seq) == 0
    last = (i % tiles_per_seq) == tiles_per_seq - 1
    win_ref[0:CONV_HALO, :] = jnp.where(first, 0.0, yp_ref[...])
    win_ref[CONV_HALO:CONV_HALO + tm, :] = ym_ref[...]
    win_ref[CONV_HALO + tm:, :] = jnp.where(last, 0.0, yn_ref[...])
    lead = CONV_HALO - width // 2
    span = CONV_RB + 2 * CONV_HALO

    def col_block(cc, carry):
        cols = pl.ds(pl.multiple_of(cc * LANES, LANES), LANES)
        for rb in range(tm // CONV_RB):
            r0 = rb * CONV_RB
            win = win_ref[r0:r0 + span, cols]
            acc = jnp.broadcast_to(bdw_ref[:, cols], (CONV_RB, LANES))
            for s in range(SUBLANES):
                taps = [k for k in range(width) if (lead + k) % SUBLANES == s]
                shifted = win if s == 0 else pltpu.roll(win, span - s, axis=0)
                for k in taps:
                    a = lead + k - s
                    acc = acc + wdw_ref[k:k + 1, cols] * shifted[a:a + CONV_RB]
            c_ref[r0:r0 + CONV_RB, cols] = acc
        return carry

    lax.fori_loop(0, c_ref.shape[1] // LANES, col_block, 0)
    y = _layernorm(c_ref[...], lng_ref[...], lnb_ref[...])
    y = (y * jax.nn.sigmoid(y)).astype(BF16)
    o_ref[...] = h_ref[...] + mod_ref[2:3, :] * _dot(y, w2_ref[...])


def _conv_out(y, w_dw, b_dw, ln_g, ln_b, w_pw2, layer, h, mods, grp, seq_len, *, tm_pref=256):
    rows, d = h.shape
    width = w_dw.shape[0]
    tm = _tile(seq_len, tm_pref)
    hb = tm // CONV_HALO
    n_halo = rows // CONV_HALO
    wpad = jnp.zeros((8 * pl.cdiv(width, 8), d), F32).at[:width].set(w_dw)
    kern = functools.partial(_conv_kernel, tm=tm, tiles_per_seq=seq_len // tm, width=width)
    return pl.pallas_call(
        kern,
        grid=(rows // tm,),
        in_specs=[
            pl.BlockSpec((CONV_HALO, d), lambda i: (jnp.maximum(i * hb - 1, 0), 0)),
            pl.BlockSpec((tm, d), lambda i: (i, 0)),
            pl.BlockSpec((CONV_HALO, d), lambda i: (jnp.minimum((i + 1) * hb, n_halo - 1), 0)),
            pl.BlockSpec(wpad.shape, lambda i: (0, 0)),
            pl.BlockSpec((1, d), lambda i: (0, 0)),
            pl.BlockSpec((1, d), lambda i: (0, 0)),
            pl.BlockSpec((1, d), lambda i: (0, 0)),
            pl.BlockSpec((None, d, d), lambda i: (layer, 0, 0)),
            pl.BlockSpec((tm, d), lambda i: (i, 0)),
            pl.BlockSpec((None, 6, d), lambda i: (grp(i, tm), 0, 0)),
        ],
        out_specs=pl.BlockSpec((tm, d), lambda i: (i, 0)),
        out_shape=jax.ShapeDtypeStruct((rows, d), F32),
        scratch_shapes=[pltpu.VMEM((tm + 2 * CONV_HALO, d), F32), pltpu.VMEM((tm, d), F32)],
        compiler_params=_params("parallel"),
        name="conv_out",
    )(y, y, y, wpad, b_dw.reshape(1, d), ln_g.reshape(1, d), ln_b.reshape(1, d), w_pw2, h, mods)


def kernel(x, c, ctx, c_ctx, ada_w, ada_b, g_mix, g_ffn, ffn_w1, ffn_w3, ffn_w2,
           a_w_in, a_ln_g, a_ln_b, a_w_s, a_b_s, a_w_out,
           b_w_qkv, b_rpb, b_w_out,
           c_w_pw1, c_w_dw, c_b_dw, c_ln_g, c_ln_b, c_w_pw2, g_final):
    bsz, seq_len, d = x.shape
    ctx_len = ctx.shape[1]
    depth = ada_w.shape[0]
    assert seq_len % (NA_QR * GRID_W) == 0 and seq_len // GRID_W >= NA_KR
    assert seq_len % CHUNK == 0 and ctx_len % CHUNK == 0 and d % LANES == 0
    assert c_w_dw.shape[1] // 2 <= CONV_HALO

    bf = lambda w: w.astype(BF16)
    ffn_w1, ffn_w3, ffn_w2 = bf(ffn_w1), bf(ffn_w3), bf(ffn_w2)
    a_w_in, a_w_s, a_w_out = bf(a_w_in), bf(a_w_s), bf(a_w_out)
    b_w_qkv, b_w_out = bf(b_w_qkv), bf(b_w_out)
    c_w_pw1, c_w_pw2 = bf(c_w_pw1), bf(c_w_pw2)

    mods = _ada_mods(c, c_ctx, ada_w, ada_b)
    lat_grp = _Group(seq_len)
    ctx_grp = _Group(ctx_len, ctx_row=bsz)
    streams = [(lat_grp, seq_len), (ctx_grp, ctx_len)]

    h = [x.reshape(bsz * seq_len, d), ctx.reshape(bsz * ctx_len, d)]
    for i in range(depth):
        last = i == depth - 1
        mixer, j = i % N_MIXERS, i // N_MIXERS
        m = mods[i]
        ctx_live = any(l % N_MIXERS == 1 for l in range(i + 1, depth))
        active = [0, 1] if ctx_live else [0]

        if mixer == 0:
            e = a_w_in.shape[2] // 2
            b_full = jnp.broadcast_to(a_b_s[j].T[:, :, None], (CHUNK, e // GROUP_DIM, GROUP_DIM)).reshape(CHUNK, e)
            for s in active:
                grp, _ = streams[s]
                z = _nm_matmul(h[s], g_mix[i], m, grp, a_w_in, j, mode="gelu", out_dtype=F32)
                h[s] = _gmlp_gate_out(z, a_ln_g[j], a_ln_b[j], a_w_s, b_full, a_w_out, j, h[s], m, grp)
        elif mixer == 1:
            qkv = [_nm_matmul(h[s], g_mix[i], m, streams[s][0], b_w_qkv, j, mode="qkv", out_dtype=BF16)
                   for s in (0, 1)]
            bias = _na_bias_tables(b_rpb[j], seq_len // GRID_W)
            o_lat = _na_lat(qkv[0], qkv[1], bias, bsz, seq_len, ctx_len)
            h[0] = _mm_residual(o_lat, b_w_out, j, h[0], m, lat_grp)
            if ctx_live:
                o_ctx = _na_ctx(qkv[1], bsz, ctx_len)
                h[1] = _mm_residual(o_ctx, b_w_out, j, h[1], m, ctx_grp)
        else:
            for s in active:
                grp, slen = streams[s]
                y = _nm_matmul(h[s], g_mix[i], m, grp, c_w_pw1, j, mode="glu", out_dtype=F32, tn_pref=512)
                h[s] = _conv_out(y, c_w_dw[j], c_b_dw[j], c_ln_g[j], c_ln_b[j], c_w_pw2, j, h[s], m, grp, slen)

        for s in active:
            grp, _ = streams[s]
            h[s] = _ffn(h[s], g_ffn[i], m, grp, ffn_w1, ffn_w3, ffn_w2, i, g_final,
                        final_norm=last and s == 0)
    return h[0].reshape(bsz, seq_len, d)
```

```python
import functools

import jax
import jax.numpy as jnp
from jax import lax
from jax.experimental import pallas as pl
from jax.experimental.pallas import tpu as pltpu

GRID_W = 64
CHUNK = 128
GROUP_DIM = 128
HEAD_DIM = 128
NA_MAX_ROWS = 8
NA_COLS = 16
N_MIXERS = 3
EPS = 1e-6
NEG_INF = -1e30

NA_QR = 4
NA_KR = NA_QR + NA_MAX_ROWS
NA_BLOCKS_PER_ITER = 2
CONV_HALO = 16
CONV_RB = 64
NORM_CHUNK_ROWS = 256
GMLP_BLOCK_ROWS = 256
FFN_CHUNK_ROWS = 256
LANES = 128
SUBLANES = 8
VMEM_LIMIT_BYTES = 60 * 1024 * 1024

F32 = jnp.float32
BF16 = jnp.bfloat16


def _params(*sem):
    return pltpu.CompilerParams(dimension_semantics=sem, vmem_limit_bytes=VMEM_LIMIT_BYTES)


def _tile(n, pref):
    t = min(n, pref)
    while n % t:
        t -= LANES
    return t


def _dot(a, b):
    return jnp.dot(a, b, preferred_element_type=F32)


def _dot_nt(a, b):
    return lax.dot_general(a, b, (((1,), (1,)), ((), ())), preferred_element_type=F32)


def _rms_modulate(x, g, shift, scale):
    gain = g * (1.0 + scale)
    return (x * lax.rsqrt(jnp.mean(x * x, axis=-1, keepdims=True) + EPS)) * gain + shift


def _layernorm(x, g, b):
    mu = jnp.mean(x, axis=-1, keepdims=True)
    xc = x - mu
    var = jnp.mean(xc * xc, axis=-1, keepdims=True)
    return xc * lax.rsqrt(var + EPS) * g + b


def _ada_kernel(s_ref, w_ref, b_ref, o_ref):
    s = s_ref[...]
    a = (s * jax.nn.sigmoid(s)).astype(BF16)
    o_ref[...] = _dot(a, w_ref[...].astype(BF16)) + b_ref[...]


def _ada_mods(c, c_ctx, ada_w, ada_b):
    depth, d, n = ada_w.shape
    bsz = c.shape[0]
    rows = 8 * pl.cdiv(bsz + 1, 8)
    s = jnp.zeros((rows, d), F32).at[:bsz].set(c).at[bsz].set(c_ctx)
    tn = _tile(n, 1024)
    out = pl.pallas_call(
        _ada_kernel,
        grid=(depth, n // tn),
        in_specs=[
            pl.BlockSpec((rows, d), lambda l, j: (0, 0)),
            pl.BlockSpec((None, d, tn), lambda l, j: (l, 0, j)),
            pl.BlockSpec((None, 1, tn), lambda l, j: (l, 0, j)),
        ],
        out_specs=pl.BlockSpec((None, rows, tn), lambda l, j: (l, 0, j)),
        out_shape=jax.ShapeDtypeStruct((depth, rows, n), F32),
        compiler_params=_params("arbitrary", "arbitrary"),
        name="ada_mod",
    )(s, ada_w, ada_b.reshape(depth, 1, n))
    return out[:, :bsz + 1].reshape(depth, bsz + 1, 6, d)


def _nm_matmul_kernel(x_ref, g_ref, mod_ref, *rest, mode, n_scaled, scale):
    if mode == "glu":
        wa_ref, wg_ref, o_ref, xn_ref = rest
    else:
        w_ref, o_ref, xn_ref = rest
    j = pl.program_id(1)

    def project(xn):
        if mode == "gelu":
            o = jax.nn.gelu(_dot(xn, w_ref[...]))
        elif mode == "glu":
            o = _dot(xn, wa_ref[...]) * jax.nn.sigmoid(_dot(xn, wg_ref[...]))
        else:
            o = _dot(xn, w_ref[...]) * jnp.where(j < n_scaled, scale, 1.0)
        return o.astype(o_ref.dtype)

    @pl.when(j == 0)
    def _():
        tm = x_ref.shape[0]
        cr = _tile(tm, NORM_CHUNK_ROWS)
        for c in range(tm // cr):
            r = pl.ds(c * cr, cr)
            xn = _rms_modulate(x_ref[r, :], g_ref[...], mod_ref[0:1, :], mod_ref[1:2, :]).astype(BF16)
            xn_ref[r, :] = xn
            o_ref[r, :] = project(xn)

    @pl.when(j > 0)
    def _():
        o_ref[...] = project(xn_ref[...])


def _nm_matmul(x, g, mods, grp, w, layer, *, mode, out_dtype, tm_pref=1024, tn_pref=1024):
    rows, d = x.shape
    n_out = w.shape[2] // 2 if mode == "glu" else w.shape[2]
    tm = grp.tile(rows, tm_pref)
    tn = _tile(d if mode == "qkv" else n_out, tn_pref)
    w_specs = [pl.BlockSpec((None, d, tn), lambda i, j: (layer, 0, j))]
    w_args = [w]
    if mode == "glu":
        off = n_out // tn
        w_specs.append(pl.BlockSpec((None, d, tn), lambda i, j: (layer, 0, j + off)))
        w_args.append(w)
    kern = functools.partial(_nm_matmul_kernel, mode=mode, n_scaled=d // tn, scale=HEAD_DIM ** -0.5)
    return pl.pallas_call(
        kern,
        grid=(rows // tm, n_out // tn),
        in_specs=[
            pl.BlockSpec((tm, d), lambda i, j: (i, 0)),
            pl.BlockSpec((1, d), lambda i, j: (0, 0)),
            pl.BlockSpec((None, 6, d), lambda i, j: (grp(i, tm), 0, 0)),
        ] + w_specs,
        out_specs=pl.BlockSpec((tm, tn), lambda i, j: (i, j)),
        out_shape=jax.ShapeDtypeStruct((rows, n_out), out_dtype),
        scratch_shapes=[pltpu.VMEM((tm, d), BF16)],
        compiler_params=_params("parallel", "arbitrary"),
        name="nm_matmul_" + mode,
    )(x, g.reshape(1, d), mods, *w_args)


class _Group:
    def __init__(self, seq_len, ctx_row=None):
        self.seq_len = seq_len
        self.ctx_row = ctx_row

    def tile(self, rows, pref):
        return _tile(rows if self.ctx_row is not None else self.seq_len, pref)

    def __call__(self, i, tm):
        if self.ctx_row is not None:
            return self.ctx_row
        return (i * tm) // self.seq_len


def _mm_residual_kernel(t_ref, w_ref, h_ref, mod_ref, o_ref):
    o_ref[...] = h_ref[...] + mod_ref[2:3, :] * _dot(t_ref[...], w_ref[...])


def _mm_residual(t, w, layer, h, mods, grp, *, tm_pref=512):
    rows, d = h.shape
    k = t.shape[1]
    tm = grp.tile(rows, tm_pref)
    return pl.pallas_call(
        _mm_residual_kernel,
        grid=(rows // tm,),
        in_specs=[
            pl.BlockSpec((tm, k), lambda i: (i, 0)),
            pl.BlockSpec((None, k, d), lambda i: (layer, 0, 0)),
            pl.BlockSpec((tm, d), lambda i: (i, 0)),
            pl.BlockSpec((None, 6, d), lambda i: (grp(i, tm), 0, 0)),
        ],
        out_specs=pl.BlockSpec((tm, d), lambda i: (i, 0)),
        out_shape=jax.ShapeDtypeStruct((rows, d), F32),
        compiler_params=_params("parallel"),
        name="mm_residual",
    )(t, w, h, mods)


def _ffn_kernel(x_ref, g_ref, mod_ref, w1_ref, w3_ref, w2_ref, gf_ref, o_ref, xn_ref, *, final_norm):
    k = pl.program_id(1)
    tm = x_ref.shape[0]

    def row_chunks(rows_per_chunk):
        cr = _tile(tm, rows_per_chunk)
        return [pl.ds(c * cr, cr) for c in range(tm // cr)]

    def partial_out(xn):
        a = _dot(xn, w1_ref[...])
        b = _dot(xn, w3_ref[...])
        return _dot((a * jax.nn.sigmoid(a) * b).astype(BF16), w2_ref[...])

    @pl.when(k == 0)
    def _():
        for r in row_chunks(NORM_CHUNK_ROWS):
            xn = _rms_modulate(x_ref[r, :], g_ref[...], mod_ref[3:4, :], mod_ref[4:5, :]).astype(BF16)
            xn_ref[r, :] = xn
            o_ref[r, :] = partial_out(xn)

    @pl.when(k > 0)
    def _():
        for r in row_chunks(FFN_CHUNK_ROWS):
            o_ref[r, :] += partial_out(xn_ref[r, :])

    @pl.when(k == pl.num_programs(1) - 1)
    def _():
        cr = _tile(tm, NORM_CHUNK_ROWS)

        def finish(c, carry):
            r = pl.ds(pl.multiple_of(c * cr, cr), cr)
            y = x_ref[r, :] + mod_ref[5:6, :] * o_ref[r, :]
            if final_norm:
                y = y * lax.rsqrt(jnp.mean(y * y, axis=-1, keepdims=True) + EPS) * gf_ref[...]
            o_ref[r, :] = y
            return carry

        lax.fori_loop(0, tm // cr, finish, 0)


def _ffn(x, g, mods, grp, w1, w3, w2, layer, g_final, *, final_norm, tm_pref=1024, tf_pref=512):
    rows, d = x.shape
    f = w1.shape[2]
    tm = grp.tile(rows, tm_pref)
    tf = _tile(f, tf_pref)
    return pl.pallas_call(
        functools.partial(_ffn_kernel, final_norm=final_norm),
        grid=(rows // tm, f // tf),
        in_specs=[
            pl.BlockSpec((tm, d), lambda i, k: (i, 0)),
            pl.BlockSpec((1, d), lambda i, k: (0, 0)),
            pl.BlockSpec((None, 6, d), lambda i, k: (grp(i, tm), 0, 0)),
            pl.BlockSpec((None, d, tf), lambda i, k: (layer, 0, k)),
            pl.BlockSpec((None, d, tf), lambda i, k: (layer, 0, k)),
            pl.BlockSpec((None, tf, d), lambda i, k: (layer, k, 0)),
            pl.BlockSpec((1, d), lambda i, k: (0, 0)),
        ],
        out_specs=pl.BlockSpec((tm, d), lambda i, k: (i, 0)),
        out_shape=jax.ShapeDtypeStruct((rows, d), F32),
        scratch_shapes=[pltpu.VMEM((tm, d), BF16)],
        compiler_params=_params("parallel", "arbitrary"),
        name="ffn",
    )(x, g.reshape(1, d), mods, w1, w3, w2, g_final.reshape(1, d))


def _gmlp_kernel(u_ref, v_ref, lng_ref, lnb_ref, ws_ref, bs_ref, wout_ref, h_ref, mod_ref, o_ref,
                 vn_ref, t_ref, *, tm, groups):
    rb = _tile(tm, GMLP_BLOCK_ROWS)
    for b in range(tm // rb):
        rows = pl.ds(b * rb, rb)
        vn_ref[rows, :] = _layernorm(v_ref[rows, :], lng_ref[...], lnb_ref[...]).astype(BF16)
        chunks = [pl.ds(b * rb + c * CHUNK, CHUNK) for c in range(rb // CHUNK)]
        for g in range(groups):
            cols = pl.ds(g * GROUP_DIM, GROUP_DIM)
            sv = _dot(ws_ref[g], jnp.concatenate([vn_ref[r, cols] for r in chunks], axis=1))
            for c, r in enumerate(chunks):
                gate = sv[:, c * GROUP_DIM:(c + 1) * GROUP_DIM] + bs_ref[:, cols]
                t_ref[r, cols] = (u_ref[r, cols] * gate).astype(BF16)
        o_ref[rows, :] = h_ref[rows, :] + mod_ref[2:3, :] * _dot(t_ref[rows, :], wout_ref[...])


def _gmlp_gate_out(z, ln_g, ln_b, w_s, b_full, w_out, layer, h, mods, grp, *, tm_pref=512):
    rows, d = h.shape
    e = z.shape[1] // 2
    groups = e // GROUP_DIM
    tm = grp.tile(rows, tm_pref)
    return pl.pallas_call(
        functools.partial(_gmlp_kernel, tm=tm, groups=groups),
        grid=(rows // tm,),
        in_specs=[
            pl.BlockSpec((tm, e), lambda i: (i, 0)),
            pl.BlockSpec((tm, e), lambda i: (i, 1)),
            pl.BlockSpec((1, e), lambda i: (0, 0)),
            pl.BlockSpec((1, e), lambda i: (0, 0)),
            pl.BlockSpec((None, groups, CHUNK, CHUNK), lambda i: (layer, 0, 0, 0)),
            pl.BlockSpec((CHUNK, e), lambda i: (0, 0)),
            pl.BlockSpec((None, e, d), lambda i: (layer, 0, 0), pipeline_mode=pl.Buffered(1)),
            pl.BlockSpec((tm, d), lambda i: (i, 0)),
            pl.BlockSpec((None, 6, d), lambda i: (grp(i, tm), 0, 0)),
        ],
        out_specs=pl.BlockSpec((tm, d), lambda i: (i, 0)),
        out_shape=jax.ShapeDtypeStruct((rows, d), F32),
        scratch_shapes=[pltpu.VMEM((tm, e), BF16), pltpu.VMEM((tm, e), BF16)],
        compiler_params=_params("parallel"),
        name="gmlp_gate_out",
    )(z, z, ln_g.reshape(1, e), ln_b.reshape(1, e), w_s, b_full, w_out, h, mods)


def _na_bias_tables(rpb, rows):
    kr = min(NA_MAX_ROWS, rows)
    nblk = rows // NA_QR
    cols = jnp.arange(GRID_W)
    c_start = jnp.clip(cols - NA_COLS // 2, 0, GRID_W - NA_COLS)
    col_ok = (cols[None, :] >= c_start[:, None]) & (cols[None, :] < c_start[:, None] + NA_COLS)
    dc_idx = jnp.clip(cols[None, :] - cols[:, None] + NA_COLS - 1, 0, 2 * NA_COLS - 2)
    slabs = jnp.where(col_ok[None, None], rpb[:, :, dc_idx].astype(F32), NEG_INF)
    masked = jnp.full((rpb.shape[0], GRID_W, GRID_W), NEG_INF, F32)
    tabs = []
    for m in (0, 1, nblk - 1):
        base = min(max(NA_QR * m - kr // 2, 0), rows - NA_KR)
        q_rows = []
        for ri in range(NA_QR):
            r = NA_QR * m + ri
            r_start = min(max(r - kr // 2, 0), rows - kr)
            keys = [slabs[:, base + ki - r + NA_MAX_ROWS - 1] if r_start <= base + ki < r_start + kr else masked
                    for ki in range(NA_KR)]
            q_rows.append(jnp.concatenate(keys, axis=-1))
        tabs.append(jnp.concatenate(q_rows, axis=1))
    return jnp.stack(tabs)


def _na_kernel(q_ref, k_ref, v_ref, kc_ref, vc_ref, bias_ref, o_ref, *, rows, heads_per_step):
    kr = min(NA_MAX_ROWS, rows)
    nblk = rows // NA_QR
    tq = NA_QR * GRID_W

    per_iter = NA_BLOCKS_PER_ITER if nblk % NA_BLOCKS_PER_ITER == 0 else 1

    def q_blocks(it, carry):
        chains = []
        for j in range(per_iter):
            m = it * per_iter + j
            base = jnp.clip(NA_QR * m - kr // 2, 0, rows - NA_KR) * GRID_W
            win = pl.ds(pl.multiple_of(base, tq), NA_KR * GRID_W)
            qrows = pl.ds(pl.multiple_of(m * tq, tq), tq)
            variant = jnp.where(m == 0, 0, jnp.where(m == nblk - 1, 2, 1))
            for hh in range(heads_per_step):
                chains.append((win, qrows, variant, hh, pl.ds(hh * HEAD_DIM, HEAD_DIM)))
        scores = []
        for win, qrows, variant, hh, cols in chains:
            q = q_ref[qrows, cols]
            scores.append((_dot_nt(q, k_ref[win, cols]) + bias_ref[variant, hh], _dot_nt(q, kc_ref[:, cols])))
        probs = []
        for s1, s2 in scores:
            mx = jnp.maximum(jnp.max(s1, axis=-1, keepdims=True), jnp.max(s2, axis=-1, keepdims=True))
            p1 = jnp.exp(s1 - mx)
            p2 = jnp.exp(s2 - mx)
            den = jnp.sum(p1, axis=-1, keepdims=True) + jnp.sum(p2, axis=-1, keepdims=True)
            probs.append((p1.astype(BF16), p2.astype(BF16), den))
        for (win, qrows, _, _, cols), (p1, p2, den) in zip(chains, probs):
            o = _dot(p1, v_ref[win, cols]) + _dot(p2, vc_ref[:, cols])
            o_ref[qrows, cols] = (o / den).astype(o_ref.dtype)
        return carry

    lax.fori_loop(0, nblk // per_iter, q_blocks, 0)


def _na_lat(qkv, qkv_ctx, bias, bsz, seq_len, ctx_len, *, heads_per_step=2):
    d = qkv.shape[1] // 3
    heads = d // HEAD_DIM
    rows = seq_len // GRID_W
    hp = heads_per_step
    groups = heads // hp
    wide = hp * HEAD_DIM
    return pl.pallas_call(
        functools.partial(_na_kernel, rows=rows, heads_per_step=hp),
        grid=(bsz, groups),
        in_specs=[
            pl.BlockSpec((seq_len, wide), lambda b, g: (b, g)),
            pl.BlockSpec((seq_len, wide), lambda b, g: (b, groups + g)),
            pl.BlockSpec((seq_len, wide), lambda b, g: (b, 2 * groups + g)),
            pl.BlockSpec((ctx_len, wide), lambda b, g: (b, groups + g)),
            pl.BlockSpec((ctx_len, wide), lambda b, g: (b, 2 * groups + g)),
            pl.BlockSpec((3, hp, NA_QR * GRID_W, NA_KR * GRID_W), lambda b, g: (0, g, 0, 0)),
        ],
        out_specs=pl.BlockSpec((seq_len, wide), lambda b, g: (b, g)),
        out_shape=jax.ShapeDtypeStruct((bsz * seq_len, d), BF16),
        compiler_params=_params("parallel", "parallel"),
        name="na_latent",
    )(qkv, qkv, qkv, qkv_ctx, qkv_ctx, bias)


def _ctx_attn_kernel(q_ref, k_ref, v_ref, o_ref):
    s = _dot_nt(q_ref[...], k_ref[...])
    p = jnp.exp(s - jnp.max(s, axis=-1, keepdims=True))
    den = jnp.sum(p, axis=-1, keepdims=True)
    o_ref[...] = (_dot(p.astype(BF16), v_ref[...]) / den).astype(o_ref.dtype)


def _na_ctx(qkv_ctx, bsz, ctx_len):
    d = qkv_ctx.shape[1] // 3
    heads = d // HEAD_DIM
    return pl.pallas_call(
        _ctx_attn_kernel,
        grid=(bsz, heads),
        in_specs=[
            pl.BlockSpec((ctx_len, HEAD_DIM), lambda b, h: (b, h)),
            pl.BlockSpec((ctx_len, HEAD_DIM), lambda b, h: (b, heads + h)),
            pl.BlockSpec((ctx_len, HEAD_DIM), lambda b, h: (b, 2 * heads + h)),
        ],
        out_specs=pl.BlockSpec((ctx_len, HEAD_DIM), lambda b, h: (b, h)),
        out_shape=jax.ShapeDtypeStruct((bsz * ctx_len, d), BF16),
        compiler_params=_params("parallel", "parallel"),
        name="na_context",
    )(qkv_ctx, qkv_ctx, qkv_ctx)


def _conv_kernel(yp_ref, ym_ref, yn_ref, wdw_ref, bdw_ref, lng_ref, lnb_ref, w2_ref, h_ref, mod_ref, o_ref,
                 win_ref, c_ref, *, tm, tiles_per_seq, width):
    i = pl.program_id(0)
    first = (i % tiles_per_seq) == 0
    last = (i % tiles_per_seq) == tiles_per_seq - 1
    win_ref[0:CONV_HALO, :] = jnp.where(first, 0.0, yp_ref[...])
    win_ref[CONV_HALO:CONV_HALO + tm, :] = ym_ref[...]
    win_ref[CONV_HALO + tm:, :] = jnp.where(last, 0.0, yn_ref[...])
    lead = CONV_HALO - width // 2
    span = CONV_RB + 2 * CONV_HALO

    def col_block(cc, carry):
        cols = pl.ds(pl.multiple_of(cc * LANES, LANES), LANES)
        for rb in range(tm // CONV_RB):
            r0 = rb * CONV_RB
            win = win_ref[r0:r0 + span, cols]
            acc = jnp.broadcast_to(bdw_ref[:, cols], (CONV_RB, LANES))
            for s in range(SUBLANES):
                taps = [k for k in range(width) if (lead + k) % SUBLANES == s]
                shifted = win if s == 0 else pltpu.roll(win, span - s, axis=0)
                for k in taps:
                    a = lead + k - s
                    acc = acc + wdw_ref[k:k + 1, cols] * shifted[a:a + CONV_RB]
            c_ref[r0:r0 + CONV_RB, cols] = acc
        return carry

    lax.fori_loop(0, c_ref.shape[1] // LANES, col_block, 0)
    y = _layernorm(c_ref[...], lng_ref[...], lnb_ref[...])
    y = (y * jax.nn.sigmoid(y)).astype(BF16)
    o_ref[...] = h_ref[...] + mod_ref[2:3, :] * _dot(y, w2_ref[...])


def _conv_out(y, w_dw, b_dw, ln_g, ln_b, w_pw2, layer, h, mods, grp, seq_len, *, tm_pref=256):
    rows, d = h.shape
    width = w_dw.shape[0]
    tm = _tile(seq_len, tm_pref)
    hb = tm // CONV_HALO
    n_halo = rows // CONV_HALO
    wpad = jnp.zeros((8 * pl.cdiv(width, 8), d), F32).at[:width].set(w_dw)
    kern = functools.partial(_conv_kernel, tm=tm, tiles_per_seq=seq_len // tm, width=width)
    return pl.pallas_call(
        kern,
        grid=(rows // tm,),
        in_specs=[
            pl.BlockSpec((CONV_HALO, d), lambda i: (jnp.maximum(i * hb - 1, 0), 0)),
            pl.BlockSpec((tm, d), lambda i: (i, 0)),
            pl.BlockSpec((CONV_HALO, d), lambda i: (jnp.minimum((i + 1) * hb, n_halo - 1), 0)),
            pl.BlockSpec(wpad.shape, lambda i: (0, 0)),
            pl.BlockSpec((1, d), lambda i: (0, 0)),
            pl.BlockSpec((1, d), lambda i: (0, 0)),
            pl.BlockSpec((1, d), lambda i: (0, 0)),
            pl.BlockSpec((None, d, d), lambda i: (layer, 0, 0)),
            pl.BlockSpec((tm, d), lambda i: (i, 0)),
            pl.BlockSpec((None, 6, d), lambda i: (grp(i, tm), 0, 0)),
        ],
        out_specs=pl.BlockSpec((tm, d), lambda i: (i, 0)),
        out_shape=jax.ShapeDtypeStruct((rows, d), F32),
        scratch_shapes=[pltpu.VMEM((tm + 2 * CONV_HALO, d), F32), pltpu.VMEM((tm, d), F32)],
        compiler_params=_params("parallel"),
        name="conv_out",
    )(y, y, y, wpad, b_dw.reshape(1, d), ln_g.reshape(1, d), ln_b.reshape(1, d), w_pw2, h, mods)


def kernel(x, c, ctx, c_ctx, ada_w, ada_b, g_mix, g_ffn, ffn_w1, ffn_w3, ffn_w2,
           a_w_in, a_ln_g, a_ln_b, a_w_s, a_b_s, a_w_out,
           b_w_qkv, b_rpb, b_w_out,
           c_w_pw1, c_w_dw, c_b_dw, c_ln_g, c_ln_b, c_w_pw2, g_final):
    bsz, seq_len, d = x.shape
    ctx_len = ctx.shape[1]
    depth = ada_w.shape[0]
    assert seq_len % (NA_QR * GRID_W) == 0 and seq_len // GRID_W >= NA_KR
    assert seq_len % CHUNK == 0 and ctx_len % CHUNK == 0 and d % LANES == 0
    assert c_w_dw.shape[1] // 2 <= CONV_HALO

    bf = lambda w: w.astype(BF16)
    ffn_w1, ffn_w3, ffn_w2 = bf(ffn_w1), bf(ffn_w3), bf(ffn_w2)
    a_w_in, a_w_s, a_w_out = bf(a_w_in), bf(a_w_s), bf(a_w_out)
    b_w_qkv, b_w_out = bf(b_w_qkv), bf(b_w_out)
    c_w_pw1, c_w_pw2 = bf(c_w_pw1), bf(c_w_pw2)

    mods = _ada_mods(c, c_ctx, ada_w, ada_b)
    lat_grp = _Group(seq_len)
    ctx_grp = _Group(ctx_len, ctx_row=bsz)
    streams = [(lat_grp, seq_len), (ctx_grp, ctx_len)]

    h = [x.reshape(bsz * seq_len, d), ctx.reshape(bsz * ctx_len, d)]
    for i in range(depth):
        last = i == depth - 1
        mixer, j = i % N_MIXERS, i // N_MIXERS
        m = mods[i]
        ctx_live = any(l % N_MIXERS == 1 for l in range(i + 1, depth))
        active = [0, 1] if ctx_live else [0]

        if mixer == 0:
            e = a_w_in.shape[2] // 2
            b_full = jnp.broadcast_to(a_b_s[j].T[:, :, None], (CHUNK, e // GROUP_DIM, GROUP_DIM)).reshape(CHUNK, e)
            for s in active:
                grp, _ = streams[s]
                z = _nm_matmul(h[s], g_mix[i], m, grp, a_w_in, j, mode="gelu", out_dtype=F32)
                h[s] = _gmlp_gate_out(z, a_ln_g[j], a_ln_b[j], a_w_s, b_full, a_w_out, j, h[s], m, grp)
        elif mixer == 1:
            qkv = [_nm_matmul(h[s], g_mix[i], m, streams[s][0], b_w_qkv, j, mode="qkv", out_dtype=BF16)
                   for s in (0, 1)]
            bias = _na_bias_tables(b_rpb[j], seq_len // GRID_W)
            o_lat = _na_lat(qkv[0], qkv[1], bias, bsz, seq_len, ctx_len)
            h[0] = _mm_residual(o_lat, b_w_out, j, h[0], m, lat_grp)
            if ctx_live:
                o_ctx = _na_ctx(qkv[1], bsz, ctx_len)
                h[1] = _mm_residual(o_ctx, b_w_out, j, h[1], m, ctx_grp)
        else:
            for s in active:
                grp, slen = streams[s]
                y = _nm_matmul(h[s], g_mix[i], m, grp, c_w_pw1, j, mode="glu", out_dtype=F32, tn_pref=512)
                h[s] = _conv_out(y, c_w_dw[j], c_b_dw[j], c_ln_g[j], c_ln_b[j], c_w_pw2, j, h[s], m, grp, slen)

        for s in active:
            grp, _ = streams[s]
            h[s] = _ffn(h[s], g_ffn[i], m, grp, ffn_w1, ffn_w3, ffn_w2, i, g_final,
                        final_norm=last and s == 0)
    return h[0].reshape(bsz, seq_len, d)
```

```python
import functools

import jax
import jax.numpy as jnp
from jax import lax
from jax.experimental import pallas as pl
from jax.experimental.pallas import tpu as pltpu

GRID_W = 64
CHUNK = 128
GROUP_DIM = 128
HEAD_DIM = 128
NA_MAX_ROWS = 8
NA_COLS = 16
N_MIXERS = 3
EPS = 1e-6
NEG_INF = -1e30

NA_QR = 4
NA_KR = NA_QR + NA_MAX_ROWS
NA_BLOCKS_PER_ITER = 2
CONV_HALO = 16
CONV_RB = 64
NORM_CHUNK_ROWS = 256
PRECAST_LAYERS = 3
GMLP_BLOCK_ROWS = 256
FFN_CHUNK_ROWS = 256
LANES = 128
SUBLANES = 8
VMEM_LIMIT_BYTES = 60 * 1024 * 1024

F32 = jnp.float32
BF16 = jnp.bfloat16


def _params(*sem):
    return pltpu.CompilerParams(dimension_semantics=sem, vmem_limit_bytes=VMEM_LIMIT_BYTES)


def _tile(n, pref):
    t = min(n, pref)
    while n % t:
        t -= LANES
    return t


def _dot(a, b):
    return jnp.dot(a, b, preferred_element_type=F32)


def _dot_nt(a, b):
    return lax.dot_general(a, b, (((1,), (1,)), ((), ())), preferred_element_type=F32)


def _rms_modulate(x, g, shift, scale):
    gain = g * (1.0 + scale)
    return (x * lax.rsqrt(jnp.mean(x * x, axis=-1, keepdims=True) + EPS)) * gain + shift


def _layernorm(x, g, b):
    mu = jnp.mean(x, axis=-1, keepdims=True)
    xc = x - mu
    var = jnp.mean(xc * xc, axis=-1, keepdims=True)
    return xc * lax.rsqrt(var + EPS) * g + b


def _ada_kernel(s_ref, w_ref, b_ref, o_ref):
    s = s_ref[...]
    a = (s * jax.nn.sigmoid(s)).astype(BF16)
    o_ref[...] = _dot(a, w_ref[...].astype(BF16)) + b_ref[...]


def _ada_mods(c, c_ctx, ada_w, ada_b):
    depth, d, n = ada_w.shape
    bsz = c.shape[0]
    rows = 8 * pl.cdiv(bsz + 1, 8)
    s = jnp.zeros((rows, d), F32).at[:bsz].set(c).at[bsz].set(c_ctx)
    tn = _tile(n, 1024)
    out = pl.pallas_call(
        _ada_kernel,
        grid=(depth, n // tn),
        in_specs=[
            pl.BlockSpec((rows, d), lambda l, j: (0, 0)),
            pl.BlockSpec((None, d, tn), lambda l, j: (l, 0, j)),
            pl.BlockSpec((None, 1, tn), lambda l, j: (l, 0, j)),
        ],
        out_specs=pl.BlockSpec((None, rows, tn), lambda l, j: (l, 0, j)),
        out_shape=jax.ShapeDtypeStruct((depth, rows, n), F32),
        compiler_params=_params("arbitrary", "arbitrary"),
        name="ada_mod",
    )(s, ada_w, ada_b.reshape(depth, 1, n))
    return out[:, :bsz + 1].reshape(depth, bsz + 1, 6, d)


def _nm_matmul_kernel(x_ref, g_ref, mod_ref, *rest, mode, n_scaled, scale, n_cast):
    n_w = 2 if mode == "glu" else 1
    w_refs, cast_in = rest[:n_w], rest[n_w:n_w + n_cast]
    o_ref, cast_out, xn_ref = rest[n_w + n_cast], rest[n_w + n_cast + 1:-1], rest[-1]
    if mode == "glu":
        wa_ref, wg_ref = w_refs
    else:
        w_ref, = w_refs
    j = pl.program_id(1)

    def side_casts():
        for src, dst in zip(cast_in, cast_out):
            dst[...] = src[...].astype(BF16)

    def project(xn):
        if mode == "gelu":
            o = jax.nn.gelu(_dot(xn, w_ref[...]))
        elif mode == "glu":
            o = _dot(xn, wa_ref[...]) * jax.nn.sigmoid(_dot(xn, wg_ref[...]))
        else:
            o = _dot(xn, w_ref[...]) * jnp.where(j < n_scaled, scale, 1.0)
        return o.astype(o_ref.dtype)

    @pl.when(j == 0)
    def _():
        side_casts()
        tm = x_ref.shape[0]
        cr = _tile(tm, NORM_CHUNK_ROWS)
        for c in range(tm // cr):
            r = pl.ds(c * cr, cr)
            xn = _rms_modulate(x_ref[r, :], g_ref[...], mod_ref[0:1, :], mod_ref[1:2, :]).astype(BF16)
            xn_ref[r, :] = xn
            o_ref[r, :] = project(xn)

    @pl.when(j > 0)
    def _():
        side_casts()
        o_ref[...] = project(xn_ref[...])


def _nm_matmul_grid(rows, d, n_out, grp, mode, tm_pref=1024, tn_pref=1024):
    tm = grp.tile(rows, tm_pref)
    tn = _tile(d if mode == "qkv" else n_out, tn_pref)
    return tm, tn, (rows // tm, n_out // tn)


def _cast_block(shape, grid):
    (r, c), (gi, gj) = shape, grid
    if r % gi or c % gj or (r // gi) % (2 * SUBLANES) or (c // gj) % LANES:
        return None
    return r // gi, c // gj


def _nm_matmul(x, g, mods, grp, w, *, mode, out_dtype, casts=(), tm_pref=1024, tn_pref=1024):
    rows, d = x.shape
    n_out = w.shape[1] // 2 if mode == "glu" else w.shape[1]
    tm, tn, grid = _nm_matmul_grid(rows, d, n_out, grp, mode, tm_pref, tn_pref)
    w_specs = [pl.BlockSpec((d, tn), lambda i, j: (0, j))]
    w_args = [w]
    if mode == "glu":
        off = n_out // tn
        w_specs.append(pl.BlockSpec((d, tn), lambda i, j: (0, j + off)))
        w_args.append(w)
    cast_in, cast_out, cast_shape = [], [], []
    for src, layer in casts:
        blk = _cast_block(src.shape[1:], grid)
        cast_in.append(pl.BlockSpec((None,) + blk, lambda i, j, layer=layer: (layer, i, j)))
        cast_out.append(pl.BlockSpec(blk, lambda i, j: (i, j)))
        cast_shape.append(jax.ShapeDtypeStruct(src.shape[1:], BF16))
    kern = functools.partial(_nm_matmul_kernel, mode=mode, n_scaled=d // tn, scale=HEAD_DIM ** -0.5,
                             n_cast=len(casts))
    out = pl.pallas_call(
        kern,
        grid=grid,
        in_specs=[
            pl.BlockSpec((tm, d), lambda i, j: (i, 0)),
            pl.BlockSpec((1, d), lambda i, j: (0, 0)),
            pl.BlockSpec((None, 6, d), lambda i, j: (grp(i, tm), 0, 0)),
        ] + w_specs + cast_in,
        out_specs=[pl.BlockSpec((tm, tn), lambda i, j: (i, j))] + cast_out,
        out_shape=[jax.ShapeDtypeStruct((rows, n_out), out_dtype)] + cast_shape,
        scratch_shapes=[pltpu.VMEM((tm, d), BF16)],
        compiler_params=_params("parallel", "arbitrary"),
        name="nm_matmul_" + mode,
    )(x, g.reshape(1, d), mods, *w_args, *[src for src, _ in casts])
    return out[0], list(out[1:])


class _Group:
    def __init__(self, seq_len, ctx_row=None):
        self.seq_len = seq_len
        self.ctx_row = ctx_row

    def tile(self, rows, pref):
        return _tile(rows if self.ctx_row is not None else self.seq_len, pref)

    def __call__(self, i, tm):
        if self.ctx_row is not None:
            return self.ctx_row
        return (i * tm) // self.seq_len


def _mm_residual_kernel(t_ref, w_ref, h_ref, mod_ref, o_ref):
    o_ref[...] = h_ref[...] + mod_ref[2:3, :] * _dot(t_ref[...], w_ref[...])


def _mm_residual(t, w, h, mods, grp, *, tm_pref=512):
    rows, d = h.shape
    k = t.shape[1]
    tm = grp.tile(rows, tm_pref)
    return pl.pallas_call(
        _mm_residual_kernel,
        grid=(rows // tm,),
        in_specs=[
            pl.BlockSpec((tm, k), lambda i: (i, 0)),
            pl.BlockSpec((k, d), lambda i: (0, 0)),
            pl.BlockSpec((tm, d), lambda i: (i, 0)),
            pl.BlockSpec((None, 6, d), lambda i: (grp(i, tm), 0, 0)),
        ],
        out_specs=pl.BlockSpec((tm, d), lambda i: (i, 0)),
        out_shape=jax.ShapeDtypeStruct((rows, d), F32),
        compiler_params=_params("parallel"),
        name="mm_residual",
    )(t, w, h, mods)


def _ffn_kernel(x_ref, g_ref, mod_ref, w1_ref, w3_ref, w2_ref, gf_ref, *rest, final_norm, cast_next):
    if cast_next:
        n1_ref, n3_ref, n2_ref, o_ref, c1_ref, c3_ref, c2_ref, xn_ref = rest
    else:
        o_ref, xn_ref = rest
    k = pl.program_id(1)
    tm = x_ref.shape[0]

    def side_casts():
        if cast_next:
            c1_ref[...] = n1_ref[...].astype(BF16)
            c3_ref[...] = n3_ref[...].astype(BF16)
            c2_ref[...] = n2_ref[...].astype(BF16)

    def row_chunks(rows_per_chunk):
        cr = _tile(tm, rows_per_chunk)
        return [pl.ds(c * cr, cr) for c in range(tm // cr)]

    def partial_out(xn):
        a = _dot(xn, w1_ref[...])
        b = _dot(xn, w3_ref[...])
        return _dot((a * jax.nn.sigmoid(a) * b).astype(BF16), w2_ref[...])

    @pl.when(k == 0)
    def _():
        side_casts()
        for r in row_chunks(NORM_CHUNK_ROWS):
            xn = _rms_modulate(x_ref[r, :], g_ref[...], mod_ref[3:4, :], mod_ref[4:5, :]).astype(BF16)
            xn_ref[r, :] = xn
            o_ref[r, :] = partial_out(xn)

    @pl.when(k > 0)
    def _():
        side_casts()
        for r in row_chunks(FFN_CHUNK_ROWS):
            o_ref[r, :] += partial_out(xn_ref[r, :])

    @pl.when(k == pl.num_programs(1) - 1)
    def _():
        cr = _tile(tm, NORM_CHUNK_ROWS)

        def finish(c, carry):
            r = pl.ds(pl.multiple_of(c * cr, cr), cr)
            y = x_ref[r, :] + mod_ref[5:6, :] * o_ref[r, :]
            if final_norm:
                y = y * lax.rsqrt(jnp.mean(y * y, axis=-1, keepdims=True) + EPS) * gf_ref[...]
            o_ref[r, :] = y
            return carry

        lax.fori_loop(0, tm // cr, finish, 0)


def _ffn_tiles(rows, d, f, grp, tm_pref=1024, tf_pref=512):
    tm = grp.tile(rows, tm_pref)
    return tm, _tile(f, tf_pref)


def _ffn_can_cast(rows, d, f, grp):
    tm, _ = _ffn_tiles(rows, d, f, grp)
    n_tiles = rows // tm
    return d % n_tiles == 0 and (d // n_tiles) % LANES == 0


def _ffn(x, g, mods, grp, w1, w3, w2, g_final, *, final_norm, next_f32=None):
    rows, d = x.shape
    f = w1.shape[1]
    tm, tf = _ffn_tiles(rows, d, f, grp)
    in_specs = [
        pl.BlockSpec((tm, d), lambda i, k: (i, 0)),
        pl.BlockSpec((1, d), lambda i, k: (0, 0)),
        pl.BlockSpec((None, 6, d), lambda i, k: (grp(i, tm), 0, 0)),
        pl.BlockSpec((d, tf), lambda i, k: (0, k)),
        pl.BlockSpec((d, tf), lambda i, k: (0, k)),
        pl.BlockSpec((tf, d), lambda i, k: (k, 0)),
        pl.BlockSpec((1, d), lambda i, k: (0, 0)),
    ]
    args = [x, g.reshape(1, d), mods, w1, w3, w2, g_final.reshape(1, d)]
    out_specs = [pl.BlockSpec((tm, d), lambda i, k: (i, 0))]
    out_shape = [jax.ShapeDtypeStruct((rows, d), F32)]
    if next_f32 is not None:
        n1, n3, n2, nxt = next_f32
        rb = d // (rows // tm)
        in_specs += [
            pl.BlockSpec((None, rb, tf), lambda i, k: (nxt, i, k)),
            pl.BlockSpec((None, rb, tf), lambda i, k: (nxt, i, k)),
            pl.BlockSpec((None, tf, rb), lambda i, k: (nxt, k, i)),
        ]
        args += [n1, n3, n2]
        out_specs += [
            pl.BlockSpec((rb, tf), lambda i, k: (i, k)),
            pl.BlockSpec((rb, tf), lambda i, k: (i, k)),
            pl.BlockSpec((tf, rb), lambda i, k: (k, i)),
        ]
        out_shape += [jax.ShapeDtypeStruct((d, f), BF16), jax.ShapeDtypeStruct((d, f), BF16),
                      jax.ShapeDtypeStruct((f, d), BF16)]
    out = pl.pallas_call(
        functools.partial(_ffn_kernel, final_norm=final_norm, cast_next=next_f32 is not None),
        grid=(rows // tm, f // tf),
        in_specs=in_specs,
        out_specs=out_specs,
        out_shape=out_shape,
        scratch_shapes=[pltpu.VMEM((tm, d), BF16)],
        compiler_params=_params("parallel", "arbitrary"),
        name="ffn",
    )(*args)
    return out[0], (tuple(out[1:]) if next_f32 is not None else None)


def _gmlp_kernel(u_ref, v_ref, lng_ref, lnb_ref, ws_ref, bs_ref, wout_ref, h_ref, mod_ref, o_ref,
                 vn_ref, t_ref, *, tm, groups):
    rb = _tile(tm, GMLP_BLOCK_ROWS)
    for b in range(tm // rb):
        rows = pl.ds(b * rb, rb)
        vn_ref[rows, :] = _layernorm(v_ref[rows, :], lng_ref[...], lnb_ref[...]).astype(BF16)
        chunks = [pl.ds(b * rb + c * CHUNK, CHUNK) for c in range(rb // CHUNK)]
        for g in range(groups):
            cols = pl.ds(g * GROUP_DIM, GROUP_DIM)
            sv = _dot(ws_ref[g], jnp.concatenate([vn_ref[r, cols] for r in chunks], axis=1))
            for c, r in enumerate(chunks):
                gate = sv[:, c * GROUP_DIM:(c + 1) * GROUP_DIM] + bs_ref[:, cols]
                t_ref[r, cols] = (u_ref[r, cols] * gate).astype(BF16)
        o_ref[rows, :] = h_ref[rows, :] + mod_ref[2:3, :] * _dot(t_ref[rows, :], wout_ref[...])


def _gmlp_gate_out(z, ln_g, ln_b, w_s, b_full, w_out, h, mods, grp, *, tm_pref=512):
    rows, d = h.shape
    e = z.shape[1] // 2
    groups = e // GROUP_DIM
    tm = grp.tile(rows, tm_pref)
    return pl.pallas_call(
        functools.partial(_gmlp_kernel, tm=tm, groups=groups),
        grid=(rows // tm,),
        in_specs=[
            pl.BlockSpec((tm, e), lambda i: (i, 0)),
            pl.BlockSpec((tm, e), lambda i: (i, 1)),
            pl.BlockSpec((1, e), lambda i: (0, 0)),
            pl.BlockSpec((1, e), lambda i: (0, 0)),
            pl.BlockSpec((groups, CHUNK, CHUNK), lambda i: (0, 0, 0)),
            pl.BlockSpec((CHUNK, e), lambda i: (0, 0)),
            pl.BlockSpec((e, d), lambda i: (0, 0), pipeline_mode=pl.Buffered(1)),
            pl.BlockSpec((tm, d), lambda i: (i, 0)),
            pl.BlockSpec((None, 6, d), lambda i: (grp(i, tm), 0, 0)),
        ],
        out_specs=pl.BlockSpec((tm, d), lambda i: (i, 0)),
        out_shape=jax.ShapeDtypeStruct((rows, d), F32),
        scratch_shapes=[pltpu.VMEM((tm, e), BF16), pltpu.VMEM((tm, e), BF16)],
        compiler_params=_params("parallel"),
        name="gmlp_gate_out",
    )(z, z, ln_g.reshape(1, e), ln_b.reshape(1, e), w_s, b_full, w_out, h, mods)


def _na_bias_tables(rpb, rows):
    kr = min(NA_MAX_ROWS, rows)
    nblk = rows // NA_QR
    cols = jnp.arange(GRID_W)
    c_start = jnp.clip(cols - NA_COLS // 2, 0, GRID_W - NA_COLS)
    col_ok = (cols[None, :] >= c_start[:, None]) & (cols[None, :] < c_start[:, None] + NA_COLS)
    dc_idx = jnp.clip(cols[None, :] - cols[:, None] + NA_COLS - 1, 0, 2 * NA_COLS - 2)
    slabs = jnp.where(col_ok[None, None], rpb[:, :, dc_idx].astype(F32), NEG_INF)
    masked = jnp.full((rpb.shape[0], GRID_W, GRID_W), NEG_INF, F32)
    tabs = []
    for m in (0, 1, nblk - 1):
        base = min(max(NA_QR * m - kr // 2, 0), rows - NA_KR)
        q_rows = []
        for ri in range(NA_QR):
            r = NA_QR * m + ri
            r_start = min(max(r - kr // 2, 0), rows - kr)
            keys = [slabs[:, base + ki - r + NA_MAX_ROWS - 1] if r_start <= base + ki < r_start + kr else masked
                    for ki in range(NA_KR)]
            q_rows.append(jnp.concatenate(keys, axis=-1))
        tabs.append(jnp.concatenate(q_rows, axis=1))
    return jnp.stack(tabs)


def _na_kernel(q_ref, k_ref, v_ref, kc_ref, vc_ref, bias_ref, o_ref, *, rows, heads_per_step):
    kr = min(NA_MAX_ROWS, rows)
    nblk = rows // NA_QR
    tq = NA_QR * GRID_W

    per_iter = NA_BLOCKS_PER_ITER if nblk % NA_BLOCKS_PER_ITER == 0 else 1

    def q_blocks(it, carry):
        chains = []
        for j in range(per_iter):
            m = it * per_iter + j
            base = jnp.clip(NA_QR * m - kr // 2, 0, rows - NA_KR) * GRID_W
            win = pl.ds(pl.multiple_of(base, tq), NA_KR * GRID_W)
            qrows = pl.ds(pl.multiple_of(m * tq, tq), tq)
            variant = jnp.where(m == 0, 0, jnp.where(m == nblk - 1, 2, 1))
            for hh in range(heads_per_step):
                chains.append((win, qrows, variant, hh, pl.ds(hh * HEAD_DIM, HEAD_DIM)))
        scores = []
        for win, qrows, variant, hh, cols in chains:
            q = q_ref[qrows, cols]
            scores.append((_dot_nt(q, k_ref[win, cols]) + bias_ref[variant, hh], _dot_nt(q, kc_ref[:, cols])))
        probs = []
        for s1, s2 in scores:
            mx = jnp.maximum(jnp.max(s1, axis=-1, keepdims=True), jnp.max(s2, axis=-1, keepdims=True))
            p1 = jnp.exp(s1 - mx)
            p2 = jnp.exp(s2 - mx)
            den = jnp.sum(p1, axis=-1, keepdims=True) + jnp.sum(p2, axis=-1, keepdims=True)
            probs.append((p1.astype(BF16), p2.astype(BF16), den))
        for (win, qrows, _, _, cols), (p1, p2, den) in zip(chains, probs):
            o = _dot(p1, v_ref[win, cols]) + _dot(p2, vc_ref[:, cols])
            o_ref[qrows, cols] = (o / den).astype(o_ref.dtype)
        return carry

    lax.fori_loop(0, nblk // per_iter, q_blocks, 0)


def _na_lat(qkv, qkv_ctx, bias, bsz, seq_len, ctx_len, *, heads_per_step=2):
    d = qkv.shape[1] // 3
    heads = d // HEAD_DIM
    rows = seq_len // GRID_W
    hp = heads_per_step
    groups = heads // hp
    wide = hp * HEAD_DIM
    return pl.pallas_call(
        functools.partial(_na_kernel, rows=rows, heads_per_step=hp),
        grid=(bsz, groups),
        in_specs=[
            pl.BlockSpec((seq_len, wide), lambda b, g: (b, g)),
            pl.BlockSpec((seq_len, wide), lambda b, g: (b, groups + g)),
            pl.BlockSpec((seq_len, wide), lambda b, g: (b, 2 * groups + g)),
            pl.BlockSpec((ctx_len, wide), lambda b, g: (b, groups + g)),
            pl.BlockSpec((ctx_len, wide), lambda b, g: (b, 2 * groups + g)),
            pl.BlockSpec((3, hp, NA_QR * GRID_W, NA_KR * GRID_W), lambda b, g: (0, g, 0, 0)),
        ],
        out_specs=pl.BlockSpec((seq_len, wide), lambda b, g: (b, g)),
        out_shape=jax.ShapeDtypeStruct((bsz * seq_len, d), BF16),
        compiler_params=_params("parallel", "parallel"),
        name="na_latent",
    )(qkv, qkv, qkv, qkv_ctx, qkv_ctx, bias)


def _ctx_attn_kernel(q_ref, k_ref, v_ref, o_ref):
    s = _dot_nt(q_ref[...], k_ref[...])
    p = jnp.exp(s - jnp.max(s, axis=-1, keepdims=True))
    den = jnp.sum(p, axis=-1, keepdims=True)
    o_ref[...] = (_dot(p.astype(BF16), v_ref[...]) / den).astype(o_ref.dtype)


def _na_ctx(qkv_ctx, bsz, ctx_len):
    d = qkv_ctx.shape[1] // 3
    heads = d // HEAD_DIM
    return pl.pallas_call(
        _ctx_attn_kernel,
        grid=(bsz, heads),
        in_specs=[
            pl.BlockSpec((ctx_len, HEAD_DIM), lambda b, h: (b, h)),
            pl.BlockSpec((ctx_len, HEAD_DIM), lambda b, h: (b, heads + h)),
            pl.BlockSpec((ctx_len, HEAD_DIM), lambda b, h: (b, 2 * heads + h)),
        ],
        out_specs=pl.BlockSpec((ctx_len, HEAD_DIM), lambda b, h: (b, h)),
        out_shape=jax.ShapeDtypeStruct((bsz * ctx_len, d), BF16),
        compiler_params=_params("parallel", "parallel"),
        name="na_context",
    )(qkv_ctx, qkv_ctx, qkv_ctx)


def _conv_kernel(yp_ref, ym_ref, yn_ref, wdw_ref, bdw_ref, lng_ref, lnb_ref, w2_ref, h_ref, mod_ref, o_ref,
                 win_ref, c_ref, *, tm, tiles_per_seq, width):
    i = pl.program_id(0)
    first = (i % tiles_per_seq) == 0
    last = (i % tiles_per_seq) == tiles_per_seq - 1
    win_ref[0:CONV_HALO, :] = jnp.where(first, 0.0, yp_ref[...])
    win_ref[CONV_HALO:CONV_HALO + tm, :] = ym_ref[...]
    win_ref[CONV_HALO + tm:, :] = jnp.where(last, 0.0, yn_ref[...])
    lead = CONV_HALO - width // 2
    span = CONV_RB + 2 * CONV_HALO

    def col_block(cc, carry):
        cols = pl.ds(pl.multiple_of(cc * LANES, LANES), LANES)
        for rb in range(tm // CONV_RB):
            r0 = rb * CONV_RB
            win = win_ref[r0:r0 + span, cols]
            acc = jnp.broadcast_to(bdw_ref[:, cols], (CONV_RB, LANES))
            for s in range(SUBLANES):
                taps = [k for k in range(width) if (lead + k) % SUBLANES == s]
                shifted = win if s == 0 else pltpu.roll(win, span - s, axis=0)
                for k in taps:
                    a = lead + k - s
                    acc = acc + wdw_ref[k:k + 1, cols] * shifted[a:a + CONV_RB]
            c_ref[r0:r0 + CONV_RB, cols] = acc
        return carry

    lax.fori_loop(0, c_ref.shape[1] // LANES, col_block, 0)
    y = _layernorm(c_ref[...], lng_ref[...], lnb_ref[...])
    y = (y * jax.nn.sigmoid(y)).astype(BF16)
    o_ref[...] = h_ref[...] + mod_ref[2:3, :] * _dot(y, w2_ref[...])


def _conv_out(y, w_dw, b_dw, ln_g, ln_b, w_pw2, h, mods, grp, seq_len, *, tm_pref=256):
    rows, d = h.shape
    width = w_dw.shape[0]
    tm = _tile(seq_len, tm_pref)
    hb = tm // CONV_HALO
    n_halo = rows // CONV_HALO
    wpad = jnp.zeros((8 * pl.cdiv(width, 8), d), F32).at[:width].set(w_dw)
    kern = functools.partial(_conv_kernel, tm=tm, tiles_per_seq=seq_len // tm, width=width)
    return pl.pallas_call(
        kern,
        grid=(rows // tm,),
        in_specs=[
            pl.BlockSpec((CONV_HALO, d), lambda i: (jnp.maximum(i * hb - 1, 0), 0)),
            pl.BlockSpec((tm, d), lambda i: (i, 0)),
            pl.BlockSpec((CONV_HALO, d), lambda i: (jnp.minimum((i + 1) * hb, n_halo - 1), 0)),
            pl.BlockSpec(wpad.shape, lambda i: (0, 0)),
            pl.BlockSpec((1, d), lambda i: (0, 0)),
            pl.BlockSpec((1, d), lambda i: (0, 0)),
            pl.BlockSpec((1, d), lambda i: (0, 0)),
            pl.BlockSpec((d, d), lambda i: (0, 0)),
            pl.BlockSpec((tm, d), lambda i: (i, 0)),
            pl.BlockSpec((None, 6, d), lambda i: (grp(i, tm), 0, 0)),
        ],
        out_specs=pl.BlockSpec((tm, d), lambda i: (i, 0)),
        out_shape=jax.ShapeDtypeStruct((rows, d), F32),
        scratch_shapes=[pltpu.VMEM((tm + 2 * CONV_HALO, d), F32), pltpu.VMEM((tm, d), F32)],
        compiler_params=_params("parallel"),
        name="conv_out",
    )(y, y, y, wpad, b_dw.reshape(1, d), ln_g.reshape(1, d), ln_b.reshape(1, d), w_pw2, h, mods)


def kernel(x, c, ctx, c_ctx, ada_w, ada_b, g_mix, g_ffn, ffn_w1, ffn_w3, ffn_w2,
           a_w_in, a_ln_g, a_ln_b, a_w_s, a_b_s, a_w_out,
           b_w_qkv, b_rpb, b_w_out,
           c_w_pw1, c_w_dw, c_b_dw, c_ln_g, c_ln_b, c_w_pw2, g_final):
    bsz, seq_len, d = x.shape
    ctx_len = ctx.shape[1]
    depth = ada_w.shape[0]
    assert seq_len % (NA_QR * GRID_W) == 0 and seq_len // GRID_W >= NA_KR
    assert seq_len % CHUNK == 0 and ctx_len % CHUNK == 0 and d % LANES == 0
    assert c_w_dw.shape[1] // 2 <= CONV_HALO

    stacked = dict(ffn_w1=ffn_w1, ffn_w3=ffn_w3, ffn_w2=ffn_w2, a_w_in=a_w_in, a_w_out=a_w_out,
                   b_w_qkv=b_w_qkv, b_w_out=b_w_out, c_w_pw1=c_w_pw1, c_w_pw2=c_w_pw2)
    ffn_names = ("ffn_w1", "ffn_w3", "ffn_w2")
    mixer_names = (("a_w_in", "a_w_out"), ("b_w_qkv", "b_w_out"), ("c_w_pw1", "c_w_pw2"))
    ready = {}

    def weight(name, idx):
        if (name, idx) not in ready:
            ready[(name, idx)] = stacked[name][idx].astype(BF16)
        return ready[(name, idx)]

    mods = _ada_mods(c, c_ctx, ada_w, ada_b)
    lat_grp = _Group(seq_len)
    ctx_grp = _Group(ctx_len, ctx_row=bsz)
    streams = [(lat_grp, seq_len), (ctx_grp, ctx_len)]

    h = [x.reshape(bsz * seq_len, d), ctx.reshape(bsz * ctx_len, d)]
    for i in range(depth):
        last = i == depth - 1
        mixer, j = i % N_MIXERS, i // N_MIXERS
        m = mods[i]
        ctx_live = any(l % N_MIXERS == 1 for l in range(i + 1, depth))
        active = [0, 1] if ctx_live else [0]

        if mixer == 0:
            e = a_w_in.shape[2] // 2
            b_full = jnp.broadcast_to(a_b_s[j].T[:, :, None], (CHUNK, e // GROUP_DIM, GROUP_DIM)).reshape(CHUNK, e)
            w_in, w_out, w_s = weight("a_w_in", j), weight("a_w_out", j), a_w_s[j].astype(BF16)
            for s in active:
                grp, _ = streams[s]
                hosted = []
                if i == 0 and s == 0:
                    wanted = [(n, 0) for n in ffn_names]
                    wanted += [(n, l // N_MIXERS) for l in range(1, min(depth, 1 + PRECAST_LAYERS))
                               for n in mixer_names[l % N_MIXERS]]
                    grid = _nm_matmul_grid(h[s].shape[0], d, 2 * e, grp, "gelu")[2]
                    hosted = [kw for kw in dict.fromkeys(wanted)
                              if kw not in ready and _cast_block(stacked[kw[0]].shape[1:], grid)]
                z, copies = _nm_matmul(h[s], g_mix[i], m, grp, w_in, mode="gelu", out_dtype=F32,
                                       casts=[(stacked[n], idx) for n, idx in hosted])
                ready.update(zip(hosted, copies))
                h[s] = _gmlp_gate_out(z, a_ln_g[j], a_ln_b[j], w_s, b_full, w_out, h[s], m, grp)
        elif mixer == 1:
            w_qkv, w_out = weight("b_w_qkv", j), weight("b_w_out", j)
            qkv = [_nm_matmul(h[s], g_mix[i], m, streams[s][0], w_qkv, mode="qkv", out_dtype=BF16)[0]
                   for s in (0, 1)]
            bias = _na_bias_tables(b_rpb[j], seq_len // GRID_W)
            o_lat = _na_lat(qkv[0], qkv[1], bias, bsz, seq_len, ctx_len)
            h[0] = _mm_residual(o_lat, w_out, h[0], m, lat_grp)
            if ctx_live:
                o_ctx = _na_ctx(qkv[1], bsz, ctx_len)
                h[1] = _mm_residual(o_ctx, w_out, h[1], m, ctx_grp)
        else:
            w_pw1, w_pw2 = weight("c_w_pw1", j), weight("c_w_pw2", j)
            for s in active:
                grp, slen = streams[s]
                y, _ = _nm_matmul(h[s], g_mix[i], m, grp, w_pw1, mode="glu", out_dtype=F32, tn_pref=512)
                h[s] = _conv_out(y, c_w_dw[j], c_b_dw[j], c_ln_g[j], c_ln_b[j], w_pw2, h[s], m, grp, slen)

        ffn_bf16 = [weight(n, i) for n in ffn_names]
        for s in active:
            grp, _ = streams[s]
            cast = s == 0 and not last and _ffn_can_cast(h[s].shape[0], d, ffn_w1.shape[2], grp)
            h[s], copies = _ffn(h[s], g_ffn[i], m, grp, *ffn_bf16, g_final, final_norm=last and s == 0,
                                next_f32=(ffn_w1, ffn_w3, ffn_w2, i + 1) if cast else None)
            if cast:
                ready.update(zip([(n, i + 1) for n in ffn_names], copies))
    return h[0].reshape(bsz, seq_len, d)
```

```python
import functools

import jax
import jax.numpy as jnp
from jax import lax
from jax.experimental import pallas as pl
from jax.experimental.pallas import tpu as pltpu

GRID_W = 64
CHUNK = 128
GROUP_DIM = 128
HEAD_DIM = 128
NA_MAX_ROWS = 8
NA_COLS = 16
N_MIXERS = 3
EPS = 1e-6
NEG_INF = -1e30

NA_QR = 4
NA_KR = NA_QR + NA_MAX_ROWS
NA_BLOCKS_PER_ITER = 2
CONV_HALO = 16
CONV_RB = 64
NORM_CHUNK_ROWS = 256
GMLP_BLOCK_ROWS = 256
FFN_CHUNK_ROWS = 256
LANES = 128
SUBLANES = 8
VMEM_LIMIT_BYTES = 60 * 1024 * 1024

F32 = jnp.float32
BF16 = jnp.bfloat16


def _params(*sem):
    return pltpu.CompilerParams(dimension_semantics=sem, vmem_limit_bytes=VMEM_LIMIT_BYTES)


def _tile(n, pref):
    t = min(n, pref)
    while n % t:
        t -= LANES
    return t


def _dot(a, b):
    return jnp.dot(a, b, preferred_element_type=F32)


def _dot_nt(a, b):
    return lax.dot_general(a, b, (((1,), (1,)), ((), ())), preferred_element_type=F32)


def _rms_modulate(x, g, shift, scale):
    gain = g * (1.0 + scale)
    return (x * lax.rsqrt(jnp.mean(x * x, axis=-1, keepdims=True) + EPS)) * gain + shift


def _layernorm(x, g, b):
    mu = jnp.mean(x, axis=-1, keepdims=True)
    xc = x - mu
    var = jnp.mean(xc * xc, axis=-1, keepdims=True)
    return xc * lax.rsqrt(var + EPS) * g + b


def _ada_kernel(s_ref, w_ref, b_ref, o_ref):
    s = s_ref[...]
    a = (s * jax.nn.sigmoid(s)).astype(BF16)
    o_ref[...] = _dot(a, w_ref[...].astype(BF16)) + b_ref[...]


def _ada_mods(c, c_ctx, ada_w, ada_b):
    depth, d, n = ada_w.shape
    bsz = c.shape[0]
    rows = 8 * pl.cdiv(bsz + 1, 8)
    s = jnp.zeros((rows, d), F32).at[:bsz].set(c).at[bsz].set(c_ctx)
    tn = _tile(n, 1024)
    out = pl.pallas_call(
        _ada_kernel,
        grid=(depth, n // tn),
        in_specs=[
            pl.BlockSpec((rows, d), lambda l, j: (0, 0)),
            pl.BlockSpec((None, d, tn), lambda l, j: (l, 0, j)),
            pl.BlockSpec((None, 1, tn), lambda l, j: (l, 0, j)),
        ],
        out_specs=pl.BlockSpec((None, rows, tn), lambda l, j: (l, 0, j)),
        out_shape=jax.ShapeDtypeStruct((depth, rows, n), F32),
        compiler_params=_params("arbitrary", "arbitrary"),
        name="ada_mod",
    )(s, ada_w, ada_b.reshape(depth, 1, n))
    return out[:, :bsz + 1].reshape(depth, bsz + 1, 6, d)


def _nm_matmul_kernel(x_ref, g_ref, mod_ref, *rest, mode, n_scaled, scale, n_cast):
    n_w = 2 if mode == "glu" else 1
    w_refs, cast_in = rest[:n_w], rest[n_w:n_w + n_cast]
    o_ref, cast_out, xn_ref = rest[n_w + n_cast], rest[n_w + n_cast + 1:-1], rest[-1]
    if mode == "glu":
        wa_ref, wg_ref = w_refs
    else:
        w_ref, = w_refs
    j = pl.program_id(1)

    def side_casts():
        for src, dst in zip(cast_in, cast_out):
            dst[...] = src[...].astype(BF16)

    def project(xn):
        if mode == "gelu":
            o = jax.nn.gelu(_dot(xn, w_ref[...]))
        elif mode == "glu":
            o = _dot(xn, wa_ref[...]) * jax.nn.sigmoid(_dot(xn, wg_ref[...]))
        else:
            o = _dot(xn, w_ref[...]) * jnp.where(j < n_scaled, scale, 1.0)
        return o.astype(o_ref.dtype)

    @pl.when(j == 0)
    def _():
        side_casts()
        tm = x_ref.shape[0]
        cr = _tile(tm, NORM_CHUNK_ROWS)
        for c in range(tm // cr):
            r = pl.ds(c * cr, cr)
            xn = _rms_modulate(x_ref[r, :], g_ref[...], mod_ref[0:1, :], mod_ref[1:2, :]).astype(BF16)
            xn_ref[r, :] = xn
            o_ref[r, :] = project(xn)

    @pl.when(j > 0)
    def _():
        side_casts()
        o_ref[...] = project(xn_ref[...])


def _nm_matmul_grid(rows, d, n_out, grp, mode, tm_pref=1024, tn_pref=1024):
    tm = grp.tile(rows, tm_pref)
    tn = _tile(d if mode == "qkv" else n_out, tn_pref)
    return tm, tn, (rows // tm, n_out // tn)


def _cast_block(shape, grid):
    (r, c), (gi, gj) = shape, grid
    if r % gi or c % gj or (r // gi) % (2 * SUBLANES) or (c // gj) % LANES:
        return None
    return r // gi, c // gj


def _nm_matmul(x, g, mods, grp, w, *, mode, out_dtype, casts=(), tm_pref=1024, tn_pref=1024):
    rows, d = x.shape
    n_out = w.shape[1] // 2 if mode == "glu" else w.shape[1]
    tm, tn, grid = _nm_matmul_grid(rows, d, n_out, grp, mode, tm_pref, tn_pref)
    w_specs = [pl.BlockSpec((d, tn), lambda i, j: (0, j))]
    w_args = [w]
    if mode == "glu":
        off = n_out // tn
        w_specs.append(pl.BlockSpec((d, tn), lambda i, j: (0, j + off)))
        w_args.append(w)
    cast_in, cast_out, cast_shape = [], [], []
    for src, layer in casts:
        blk = _cast_block(src.shape[1:], grid)
        cast_in.append(pl.BlockSpec((None,) + blk, lambda i, j, layer=layer: (layer, i, j)))
        cast_out.append(pl.BlockSpec(blk, lambda i, j: (i, j)))
        cast_shape.append(jax.ShapeDtypeStruct(src.shape[1:], BF16))
    kern = functools.partial(_nm_matmul_kernel, mode=mode, n_scaled=d // tn, scale=HEAD_DIM ** -0.5,
                             n_cast=len(casts))
    out = pl.pallas_call(
        kern,
        grid=grid,
        in_specs=[
            pl.BlockSpec((tm, d), lambda i, j: (i, 0)),
            pl.BlockSpec((1, d), lambda i, j: (0, 0)),
            pl.BlockSpec((None, 6, d), lambda i, j: (grp(i, tm), 0, 0)),
        ] + w_specs + cast_in,
        out_specs=[pl.BlockSpec((tm, tn), lambda i, j: (i, j))] + cast_out,
        out_shape=[jax.ShapeDtypeStruct((rows, n_out), out_dtype)] + cast_shape,
        scratch_shapes=[pltpu.VMEM((tm, d), BF16)],
        compiler_params=_params("parallel", "arbitrary"),
        name="nm_matmul_" + mode,
    )(x, g.reshape(1, d), mods, *w_args, *[src for src, _ in casts])
    return out[0], list(out[1:])


class _Group:
    def __init__(self, seq_len, ctx_row=None):
        self.seq_len = seq_len
        self.ctx_row = ctx_row

    def tile(self, rows, pref):
        return _tile(rows if self.ctx_row is not None else self.seq_len, pref)

    def __call__(self, i, tm):
        if self.ctx_row is not None:
            return self.ctx_row
        return (i * tm) // self.seq_len


def _mm_residual_kernel(t_ref, w_ref, h_ref, mod_ref, o_ref):
    o_ref[...] = h_ref[...] + mod_ref[2:3, :] * _dot(t_ref[...], w_ref[...])


def _mm_residual(t, w, h, mods, grp, *, tm_pref=512):
    rows, d = h.shape
    k = t.shape[1]
    tm = grp.tile(rows, tm_pref)
    return pl.pallas_call(
        _mm_residual_kernel,
        grid=(rows // tm,),
        in_specs=[
            pl.BlockSpec((tm, k), lambda i: (i, 0)),
            pl.BlockSpec((k, d), lambda i: (0, 0)),
            pl.BlockSpec((tm, d), lambda i: (i, 0)),
            pl.BlockSpec((None, 6, d), lambda i: (grp(i, tm), 0, 0)),
        ],
        out_specs=pl.BlockSpec((tm, d), lambda i: (i, 0)),
        out_shape=jax.ShapeDtypeStruct((rows, d), F32),
        compiler_params=_params("parallel"),
        name="mm_residual",
    )(t, w, h, mods)


def _ffn_kernel(x_ref, g_ref, mod_ref, w1_ref, w3_ref, w2_ref, gf_ref, *rest, final_norm, n_cast):
    cast_in, o_ref, cast_out, xn_ref = rest[:n_cast], rest[n_cast], rest[n_cast + 1:-1], rest[-1]
    k = pl.program_id(1)
    tm = x_ref.shape[0]

    def side_casts():
        for src, dst in zip(cast_in, cast_out):
            dst[...] = src[...].astype(BF16)

    def row_chunks(rows_per_chunk):
        cr = _tile(tm, rows_per_chunk)
        return [pl.ds(c * cr, cr) for c in range(tm // cr)]

    def partial_out(xn):
        a = _dot(xn, w1_ref[...])
        b = _dot(xn, w3_ref[...])
        return _dot((a * jax.nn.sigmoid(a) * b).astype(BF16), w2_ref[...])

    @pl.when(k == 0)
    def _():
        side_casts()
        for r in row_chunks(NORM_CHUNK_ROWS):
            xn = _rms_modulate(x_ref[r, :], g_ref[...], mod_ref[3:4, :], mod_ref[4:5, :]).astype(BF16)
            xn_ref[r, :] = xn
            o_ref[r, :] = partial_out(xn)

    @pl.when(k > 0)
    def _():
        side_casts()
        for r in row_chunks(FFN_CHUNK_ROWS):
            o_ref[r, :] += partial_out(xn_ref[r, :])

    @pl.when(k == pl.num_programs(1) - 1)
    def _():
        cr = _tile(tm, NORM_CHUNK_ROWS)

        def finish(c, carry):
            r = pl.ds(pl.multiple_of(c * cr, cr), cr)
            y = x_ref[r, :] + mod_ref[5:6, :] * o_ref[r, :]
            if final_norm:
                y = y * lax.rsqrt(jnp.mean(y * y, axis=-1, keepdims=True) + EPS) * gf_ref[...]
            o_ref[r, :] = y
            return carry

        lax.fori_loop(0, tm // cr, finish, 0)


def _ffn_tiles(rows, d, f, grp, tm_pref=1024, tf_pref=512):
    tm = grp.tile(rows, tm_pref)
    return tm, _tile(f, tf_pref)


def _ffn_cast_plan(shape, rows, d, f, grp):
    tm, tf = _ffn_tiles(rows, d, f, grp)
    n_tiles, n_steps = rows // tm, f // tf
    r, c = shape
    aligned = lambda blk: blk[0] % (2 * SUBLANES) == 0 and blk[1] % LANES == 0
    if c == f and r % n_tiles == 0 and aligned((r // n_tiles, tf)):
        return (r // n_tiles, tf), lambda i, k: (i, k)
    if r == f and c % n_tiles == 0 and aligned((tf, c // n_tiles)):
        return (tf, c // n_tiles), lambda i, k: (k, i)
    split = 1
    while split * 2 <= n_steps and _cast_block(shape, (n_tiles, split * 2)):
        split *= 2
    blk = _cast_block(shape, (n_tiles, split))
    return (blk, lambda i, k: (i, jnp.minimum(k, split - 1))) if blk else None


def _ffn(x, g, mods, grp, w1, w3, w2, g_final, *, final_norm, casts=()):
    rows, d = x.shape
    f = w1.shape[1]
    tm, tf = _ffn_tiles(rows, d, f, grp)
    in_specs = [
        pl.BlockSpec((tm, d), lambda i, k: (i, 0)),
        pl.BlockSpec((1, d), lambda i, k: (0, 0)),
        pl.BlockSpec((None, 6, d), lambda i, k: (grp(i, tm), 0, 0)),
        pl.BlockSpec((d, tf), lambda i, k: (0, k)),
        pl.BlockSpec((d, tf), lambda i, k: (0, k)),
        pl.BlockSpec((tf, d), lambda i, k: (k, 0)),
        pl.BlockSpec((1, d), lambda i, k: (0, 0)),
    ]
    args = [x, g.reshape(1, d), mods, w1, w3, w2, g_final.reshape(1, d)]
    out_specs = [pl.BlockSpec((tm, d), lambda i, k: (i, 0))]
    out_shape = [jax.ShapeDtypeStruct((rows, d), F32)]
    for src, layer in casts:
        blk, idx = _ffn_cast_plan(src.shape[1:], rows, d, f, grp)
        in_specs.append(pl.BlockSpec((None,) + blk, lambda i, k, layer=layer, idx=idx: (layer,) + idx(i, k)))
        args.append(src)
        out_specs.append(pl.BlockSpec(blk, idx))
        out_shape.append(jax.ShapeDtypeStruct(src.shape[1:], BF16))
    out = pl.pallas_call(
        functools.partial(_ffn_kernel, final_norm=final_norm, n_cast=len(casts)),
        grid=(rows // tm, f // tf),
        in_specs=in_specs,
        out_specs=out_specs,
        out_shape=out_shape,
        scratch_shapes=[pltpu.VMEM((tm, d), BF16)],
        compiler_params=_params("parallel", "arbitrary"),
        name="ffn",
    )(*args)
    return out[0], list(out[1:])


def _gmlp_kernel(u_ref, v_ref, lng_ref, lnb_ref, ws_ref, bs_ref, wout_ref, h_ref, mod_ref, o_ref,
                 vn_ref, t_ref, *, tm, groups):
    rb = _tile(tm, GMLP_BLOCK_ROWS)
    for b in range(tm // rb):
        rows = pl.ds(b * rb, rb)
        vn_ref[rows, :] = _layernorm(v_ref[rows, :], lng_ref[...], lnb_ref[...]).astype(BF16)
        chunks = [pl.ds(b * rb + c * CHUNK, CHUNK) for c in range(rb // CHUNK)]
        for g in range(groups):
            cols = pl.ds(g * GROUP_DIM, GROUP_DIM)
            sv = _dot(ws_ref[g], jnp.concatenate([vn_ref[r, cols] for r in chunks], axis=1))
            for c, r in enumerate(chunks):
                gate = sv[:, c * GROUP_DIM:(c + 1) * GROUP_DIM] + bs_ref[:, cols]
                t_ref[r, cols] = (u_ref[r, cols] * gate).astype(BF16)
        o_ref[rows, :] = h_ref[rows, :] + mod_ref[2:3, :] * _dot(t_ref[rows, :], wout_ref[...])


def _gmlp_gate_out(z, ln_g, ln_b, w_s, b_full, w_out, h, mods, grp, *, tm_pref=512):
    rows, d = h.shape
    e = z.shape[1] // 2
    groups = e // GROUP_DIM
    tm = grp.tile(rows, tm_pref)
    return pl.pallas_call(
        functools.partial(_gmlp_kernel, tm=tm, groups=groups),
        grid=(rows // tm,),
        in_specs=[
            pl.BlockSpec((tm, e), lambda i: (i, 0)),
            pl.BlockSpec((tm, e), lambda i: (i, 1)),
            pl.BlockSpec((1, e), lambda i: (0, 0)),
            pl.BlockSpec((1, e), lambda i: (0, 0)),
            pl.BlockSpec((groups, CHUNK, CHUNK), lambda i: (0, 0, 0)),
            pl.BlockSpec((CHUNK, e), lambda i: (0, 0)),
            pl.BlockSpec((e, d), lambda i: (0, 0), pipeline_mode=pl.Buffered(1)),
            pl.BlockSpec((tm, d), lambda i: (i, 0)),
            pl.BlockSpec((None, 6, d), lambda i: (grp(i, tm), 0, 0)),
        ],
        out_specs=pl.BlockSpec((tm, d), lambda i: (i, 0)),
        out_shape=jax.ShapeDtypeStruct((rows, d), F32),
        scratch_shapes=[pltpu.VMEM((tm, e), BF16), pltpu.VMEM((tm, e), BF16)],
        compiler_params=_params("parallel"),
        name="gmlp_gate_out",
    )(z, z, ln_g.reshape(1, e), ln_b.reshape(1, e), w_s, b_full, w_out, h, mods)


def _na_bias_tables(rpb, rows):
    kr = min(NA_MAX_ROWS, rows)
    nblk = rows // NA_QR
    cols = jnp.arange(GRID_W)
    c_start = jnp.clip(cols - NA_COLS // 2, 0, GRID_W - NA_COLS)
    col_ok = (cols[None, :] >= c_start[:, None]) & (cols[None, :] < c_start[:, None] + NA_COLS)
    dc_idx = jnp.clip(cols[None, :] - cols[:, None] + NA_COLS - 1, 0, 2 * NA_COLS - 2)
    slabs = jnp.where(col_ok[None, None], rpb[:, :, dc_idx].astype(F32), NEG_INF)
    n_off = slabs.shape[1]
    slabs = jnp.concatenate([slabs, jnp.full_like(slabs[:, :1], NEG_INF)], axis=1)
    pick = []
    for m in (0, 1, nblk - 1):
        base = min(max(NA_QR * m - kr // 2, 0), rows - NA_KR)
        for ri in range(NA_QR):
            r = NA_QR * m + ri
            r_start = min(max(r - kr // 2, 0), rows - kr)
            pick += [base + ki - r + NA_MAX_ROWS - 1 if r_start <= base + ki < r_start + kr else n_off
                     for ki in range(NA_KR)]
    tabs = jnp.take(slabs, jnp.asarray(pick, jnp.int32), axis=1)
    tabs = tabs.reshape(rpb.shape[0], 3, NA_QR, NA_KR, GRID_W, GRID_W).transpose(1, 0, 2, 4, 3, 5)
    return tabs.reshape(3, rpb.shape[0], NA_QR * GRID_W, NA_KR * GRID_W)


def _na_kernel(q_ref, k_ref, v_ref, kc_ref, vc_ref, bias_ref, o_ref, *, rows, heads_per_step):
    kr = min(NA_MAX_ROWS, rows)
    nblk = rows // NA_QR
    tq = NA_QR * GRID_W

    per_iter = NA_BLOCKS_PER_ITER if nblk % NA_BLOCKS_PER_ITER == 0 else 1

    def q_blocks(it, carry):
        chains = []
        for j in range(per_iter):
            m = it * per_iter + j
            base = jnp.clip(NA_QR * m - kr // 2, 0, rows - NA_KR) * GRID_W
            win = pl.ds(pl.multiple_of(base, tq), NA_KR * GRID_W)
            qrows = pl.ds(pl.multiple_of(m * tq, tq), tq)
            variant = jnp.where(m == 0, 0, jnp.where(m == nblk - 1, 2, 1))
            for hh in range(heads_per_step):
                chains.append((win, qrows, variant, hh, pl.ds(hh * HEAD_DIM, HEAD_DIM)))
        scores = []
        for win, qrows, variant, hh, cols in chains:
            q = q_ref[qrows, cols]
            scores.append((_dot_nt(q, k_ref[win, cols]) + bias_ref[variant, hh], _dot_nt(q, kc_ref[:, cols])))
        probs = []
        for s1, s2 in scores:
            mx = jnp.maximum(jnp.max(s1, axis=-1, keepdims=True), jnp.max(s2, axis=-1, keepdims=True))
            p1 = jnp.exp(s1 - mx)
            p2 = jnp.exp(s2 - mx)
            den = jnp.sum(p1, axis=-1, keepdims=True) + jnp.sum(p2, axis=-1, keepdims=True)
            probs.append((p1.astype(BF16), p2.astype(BF16), den))
        for (win, qrows, _, _, cols), (p1, p2, den) in zip(chains, probs):
            o = _dot(p1, v_ref[win, cols]) + _dot(p2, vc_ref[:, cols])
            o_ref[qrows, cols] = (o / den).astype(o_ref.dtype)
        return carry

    lax.fori_loop(0, nblk // per_iter, q_blocks, 0)


def _na_lat(qkv, qkv_ctx, bias, bsz, seq_len, ctx_len, *, heads_per_step=2):
    d = qkv.shape[1] // 3
    heads = d // HEAD_DIM
    rows = seq_len // GRID_W
    hp = heads_per_step
    groups = heads // hp
    wide = hp * HEAD_DIM
    return pl.pallas_call(
        functools.partial(_na_kernel, rows=rows, heads_per_step=hp),
        grid=(bsz, groups),
        in_specs=[
            pl.BlockSpec((seq_len, wide), lambda b, g: (b, g)),
            pl.BlockSpec((seq_len, wide), lambda b, g: (b, groups + g)),
            pl.BlockSpec((seq_len, wide), lambda b, g: (b, 2 * groups + g)),
            pl.BlockSpec((ctx_len, wide), lambda b, g: (b, groups + g)),
            pl.BlockSpec((ctx_len, wide), lambda b, g: (b, 2 * groups + g)),
            pl.BlockSpec((3, hp, NA_QR * GRID_W, NA_KR * GRID_W), lambda b, g: (0, g, 0, 0)),
        ],
        out_specs=pl.BlockSpec((seq_len, wide), lambda b, g: (b, g)),
        out_shape=jax.ShapeDtypeStruct((bsz * seq_len, d), BF16),
        compiler_params=_params("parallel", "parallel"),
        name="na_latent",
    )(qkv, qkv, qkv, qkv_ctx, qkv_ctx, bias)


def _ctx_attn_kernel(q_ref, k_ref, v_ref, o_ref):
    s = _dot_nt(q_ref[...], k_ref[...])
    p = jnp.exp(s - jnp.max(s, axis=-1, keepdims=True))
    den = jnp.sum(p, axis=-1, keepdims=True)
    o_ref[...] = (_dot(p.astype(BF16), v_ref[...]) / den).astype(o_ref.dtype)


def _na_ctx(qkv_ctx, bsz, ctx_len):
    d = qkv_ctx.shape[1] // 3
    heads = d // HEAD_DIM
    return pl.pallas_call(
        _ctx_attn_kernel,
        grid=(bsz, heads),
        in_specs=[
            pl.BlockSpec((ctx_len, HEAD_DIM), lambda b, h: (b, h)),
            pl.BlockSpec((ctx_len, HEAD_DIM), lambda b, h: (b, heads + h)),
            pl.BlockSpec((ctx_len, HEAD_DIM), lambda b, h: (b, 2 * heads + h)),
        ],
        out_specs=pl.BlockSpec((ctx_len, HEAD_DIM), lambda b, h: (b, h)),
        out_shape=jax.ShapeDtypeStruct((bsz * ctx_len, d), BF16),
        compiler_params=_params("parallel", "parallel"),
        name="na_context",
    )(qkv_ctx, qkv_ctx, qkv_ctx)


def _conv_kernel(yp_ref, ym_ref, yn_ref, wdw_ref, bdw_ref, lng_ref, lnb_ref, w2_ref, h_ref, mod_ref, o_ref,
                 win_ref, c_ref, *, tm, tiles_per_seq, width):
    i = pl.program_id(0)
    first = (i % tiles_per_seq) == 0
    last = (i % tiles_per_seq) == tiles_per_seq - 1
    win_ref[0:CONV_HALO, :] = jnp.where(first, 0.0, yp_ref[...])
    win_ref[CONV_HALO:CONV_HALO + tm, :] = ym_ref[...]
    win_ref[CONV_HALO + tm:, :] = jnp.where(last, 0.0, yn_ref[...])
    lead = CONV_HALO - width // 2
    span = CONV_RB + 2 * CONV_HALO

    def col_block(cc, carry):
        cols = pl.ds(pl.multiple_of(cc * LANES, LANES), LANES)
        for rb in range(tm // CONV_RB):
            r0 = rb * CONV_RB
            win = win_ref[r0:r0 + span, cols]
            acc = jnp.broadcast_to(bdw_ref[:, cols], (CONV_RB, LANES))
            for s in range(SUBLANES):
                taps = [k for k in range(width) if (lead + k) % SUBLANES == s]
                shifted = win if s == 0 else pltpu.roll(win, span - s, axis=0)
                for k in taps:
                    a = lead + k - s
                    acc = acc + wdw_ref[k:k + 1, cols] * shifted[a:a + CONV_RB]
            c_ref[r0:r0 + CONV_RB, cols] = acc
        return carry

    lax.fori_loop(0, c_ref.shape[1] // LANES, col_block, 0)
    y = _layernorm(c_ref[...], lng_ref[...], lnb_ref[...])
    y = (y * jax.nn.sigmoid(y)).astype(BF16)
    o_ref[...] = h_ref[...] + mod_ref[2:3, :] * _dot(y, w2_ref[...])


def _conv_out(y, w_dw, b_dw, ln_g, ln_b, w_pw2, h, mods, grp, seq_len, *, tm_pref=256):
    rows, d = h.shape
    width = w_dw.shape[0]
    tm = _tile(seq_len, tm_pref)
    hb = tm // CONV_HALO
    n_halo = rows // CONV_HALO
    wpad = jnp.zeros((8 * pl.cdiv(width, 8), d), F32).at[:width].set(w_dw)
    kern = functools.partial(_conv_kernel, tm=tm, tiles_per_seq=seq_len // tm, width=width)
    return pl.pallas_call(
        kern,
        grid=(rows // tm,),
        in_specs=[
            pl.BlockSpec((CONV_HALO, d), lambda i: (jnp.maximum(i * hb - 1, 0), 0)),
            pl.BlockSpec((tm, d), lambda i: (i, 0)),
            pl.BlockSpec((CONV_HALO, d), lambda i: (jnp.minimum((i + 1) * hb, n_halo - 1), 0)),
            pl.BlockSpec(wpad.shape, lambda i: (0, 0)),
            pl.BlockSpec((1, d), lambda i: (0, 0)),
            pl.BlockSpec((1, d), lambda i: (0, 0)),
            pl.BlockSpec((1, d), lambda i: (0, 0)),
            pl.BlockSpec((d, d), lambda i: (0, 0)),
            pl.BlockSpec((tm, d), lambda i: (i, 0)),
            pl.BlockSpec((None, 6, d), lambda i: (grp(i, tm), 0, 0)),
        ],
        out_specs=pl.BlockSpec((tm, d), lambda i: (i, 0)),
        out_shape=jax.ShapeDtypeStruct((rows, d), F32),
        scratch_shapes=[pltpu.VMEM((tm + 2 * CONV_HALO, d), F32), pltpu.VMEM((tm, d), F32)],
        compiler_params=_params("parallel"),
        name="conv_out",
    )(y, y, y, wpad, b_dw.reshape(1, d), ln_g.reshape(1, d), ln_b.reshape(1, d), w_pw2, h, mods)


def kernel(x, c, ctx, c_ctx, ada_w, ada_b, g_mix, g_ffn, ffn_w1, ffn_w3, ffn_w2,
           a_w_in, a_ln_g, a_ln_b, a_w_s, a_b_s, a_w_out,
           b_w_qkv, b_rpb, b_w_out,
           c_w_pw1, c_w_dw, c_b_dw, c_ln_g, c_ln_b, c_w_pw2, g_final):
    bsz, seq_len, d = x.shape
    ctx_len = ctx.shape[1]
    depth = ada_w.shape[0]
    assert seq_len % (NA_QR * GRID_W) == 0 and seq_len // GRID_W >= NA_KR
    assert seq_len % CHUNK == 0 and ctx_len % CHUNK == 0 and d % LANES == 0
    assert c_w_dw.shape[1] // 2 <= CONV_HALO

    stacked = dict(ffn_w1=ffn_w1, ffn_w3=ffn_w3, ffn_w2=ffn_w2, a_w_in=a_w_in, a_w_out=a_w_out,
                   b_w_qkv=b_w_qkv, b_w_out=b_w_out, c_w_pw1=c_w_pw1, c_w_pw2=c_w_pw2)
    ffn_names = ("ffn_w1", "ffn_w3", "ffn_w2")
    mixer_names = (("a_w_in", "a_w_out"), ("b_w_qkv", "b_w_out"), ("c_w_pw1", "c_w_pw2"))
    ready = {}

    def weight(name, idx):
        if (name, idx) not in ready:
            ready[(name, idx)] = stacked[name][idx].astype(BF16)
        return ready[(name, idx)]

    mods = _ada_mods(c, c_ctx, ada_w, ada_b)
    lat_grp = _Group(seq_len)
    ctx_grp = _Group(ctx_len, ctx_row=bsz)
    streams = [(lat_grp, seq_len), (ctx_grp, ctx_len)]

    h = [x.reshape(bsz * seq_len, d), ctx.reshape(bsz * ctx_len, d)]
    for i in range(depth):
        last = i == depth - 1
        mixer, j = i % N_MIXERS, i // N_MIXERS
        m = mods[i]
        ctx_live = any(l % N_MIXERS == 1 for l in range(i + 1, depth))
        active = [0, 1] if ctx_live else [0]

        if mixer == 0:
            e = a_w_in.shape[2] // 2
            b_full = jnp.broadcast_to(a_b_s[j].T[:, :, None], (CHUNK, e // GROUP_DIM, GROUP_DIM)).reshape(CHUNK, e)
            w_in, w_out, w_s = weight("a_w_in", j), weight("a_w_out", j), a_w_s[j].astype(BF16)
            for s in active:
                grp, _ = streams[s]
                hosted = []
                if i == 0 and s == 0:
                    grid = _nm_matmul_grid(h[s].shape[0], d, 2 * e, grp, "gelu")[2]
                    hosted = [(n, 0) for n in ffn_names if _cast_block(stacked[n].shape[1:], grid)]
                z, copies = _nm_matmul(h[s], g_mix[i], m, grp, w_in, mode="gelu", out_dtype=F32,
                                       casts=[(stacked[n], idx) for n, idx in hosted])
                ready.update(zip(hosted, copies))
                h[s] = _gmlp_gate_out(z, a_ln_g[j], a_ln_b[j], w_s, b_full, w_out, h[s], m, grp)
        elif mixer == 1:
            w_qkv, w_out = weight("b_w_qkv", j), weight("b_w_out", j)
            qkv = [_nm_matmul(h[s], g_mix[i], m, streams[s][0], w_qkv, mode="qkv", out_dtype=BF16)[0]
                   for s in (0, 1)]
            bias = _na_bias_tables(b_rpb[j], seq_len // GRID_W)
            o_lat = _na_lat(qkv[0], qkv[1], bias, bsz, seq_len, ctx_len)
            h[0] = _mm_residual(o_lat, w_out, h[0], m, lat_grp)
            if ctx_live:
                o_ctx = _na_ctx(qkv[1], bsz, ctx_len)
                h[1] = _mm_residual(o_ctx, w_out, h[1], m, ctx_grp)
        else:
            w_pw1, w_pw2 = weight("c_w_pw1", j), weight("c_w_pw2", j)
            for s in active:
                grp, slen = streams[s]
                y, _ = _nm_matmul(h[s], g_mix[i], m, grp, w_pw1, mode="glu", out_dtype=F32, tn_pref=512)
                h[s] = _conv_out(y, c_w_dw[j], c_b_dw[j], c_ln_g[j], c_ln_b[j], w_pw2, h[s], m, grp, slen)

        ffn_bf16 = [weight(n, i) for n in ffn_names]
        for s in active:
            grp, _ = streams[s]
            hosted = []
            if s == 0 and not last:
                nxt = i + 1
                wanted = [(n, nxt) for n in ffn_names] + [(n, nxt // N_MIXERS) for n in mixer_names[nxt % N_MIXERS]]
                hosted = [kw for kw in wanted if kw not in ready and
                          _ffn_cast_plan(stacked[kw[0]].shape[1:], h[s].shape[0], d, ffn_w1.shape[2], grp)]
            h[s], copies = _ffn(h[s], g_ffn[i], m, grp, *ffn_bf16, g_final, final_norm=last and s == 0,
                                casts=[(stacked[n], idx) for n, idx in hosted])
            ready.update(zip(hosted, copies))
    return h[0].reshape(bsz, seq_len, d)
```

```python
import functools

import jax
import jax.numpy as jnp
from jax import lax
from jax.experimental import pallas as pl
from jax.experimental.pallas import tpu as pltpu

GRID_W = 64
CHUNK = 128
GROUP_DIM = 128
HEAD_DIM = 128
NA_MAX_ROWS = 8
NA_COLS = 16
N_MIXERS = 3
EPS = 1e-6
NEG_INF = -1e30

NA_QR = 4
NA_KR = NA_QR + NA_MAX_ROWS
NA_BLOCKS_PER_ITER = 2
CONV_HALO = 16
CONV_RB = 64
NORM_CHUNK_ROWS = 256
GMLP_BLOCK_ROWS = 256
FFN_CHUNK_ROWS = 256
LANES = 128
SUBLANES = 8
NA_PAIR = LANES // GRID_W
VMEM_LIMIT_BYTES = 60 * 1024 * 1024

F32 = jnp.float32
BF16 = jnp.bfloat16


def _params(*sem):
    return pltpu.CompilerParams(dimension_semantics=sem, vmem_limit_bytes=VMEM_LIMIT_BYTES)


def _tile(n, pref):
    t = min(n, pref)
    while n % t:
        t -= LANES
    return t


def _dot(a, b):
    return jnp.dot(a, b, preferred_element_type=F32)


def _dot_nt(a, b):
    return lax.dot_general(a, b, (((1,), (1,)), ((), ())), preferred_element_type=F32)


def _rms_modulate(x, g, shift, scale):
    gain = g * (1.0 + scale)
    return (x * lax.rsqrt(jnp.mean(x * x, axis=-1, keepdims=True) + EPS)) * gain + shift


def _layernorm(x, g, b):
    mu = jnp.mean(x, axis=-1, keepdims=True)
    xc = x - mu
    var = jnp.mean(xc * xc, axis=-1, keepdims=True)
    return xc * lax.rsqrt(var + EPS) * g + b


def _ada_kernel(s_ref, w_ref, b_ref, o_ref):
    s = s_ref[...]
    a = (s * jax.nn.sigmoid(s)).astype(BF16)
    o_ref[...] = _dot(a, w_ref[...].astype(BF16)) + b_ref[...]


def _ada_mods(c, c_ctx, ada_w, ada_b):
    depth, d, n = ada_w.shape
    bsz = c.shape[0]
    rows = 8 * pl.cdiv(bsz + 1, 8)
    s = jnp.zeros((rows, d), F32).at[:bsz].set(c).at[bsz].set(c_ctx)
    tn = _tile(n, 1024)
    out = pl.pallas_call(
        _ada_kernel,
        grid=(depth, n // tn),
        in_specs=[
            pl.BlockSpec((rows, d), lambda l, j: (0, 0)),
            pl.BlockSpec((None, d, tn), lambda l, j: (l, 0, j)),
            pl.BlockSpec((None, 1, tn), lambda l, j: (l, 0, j)),
        ],
        out_specs=pl.BlockSpec((None, rows, tn), lambda l, j: (l, 0, j)),
        out_shape=jax.ShapeDtypeStruct((depth, rows, n), F32),
        compiler_params=_params("arbitrary", "arbitrary"),
        name="ada_mod",
    )(s, ada_w, ada_b.reshape(depth, 1, n))
    return out[:, :bsz + 1].reshape(depth, bsz + 1, 6, d)


def _nm_matmul_kernel(x_ref, g_ref, mod_ref, *rest, mode, n_scaled, scale, n_cast):
    n_w = 2 if mode == "glu" else 1
    w_refs, cast_in = rest[:n_w], rest[n_w:n_w + n_cast]
    o_ref, cast_out, xn_ref = rest[n_w + n_cast], rest[n_w + n_cast + 1:-1], rest[-1]
    if mode == "glu":
        wa_ref, wg_ref = w_refs
    else:
        w_ref, = w_refs
    j = pl.program_id(1)

    def side_casts():
        for src, dst in zip(cast_in, cast_out):
            dst[...] = src[...].astype(BF16)

    def project(xn):
        if mode == "gelu":
            o = jax.nn.gelu(_dot(xn, w_ref[...]))
        elif mode == "glu":
            o = _dot(xn, wa_ref[...]) * jax.nn.sigmoid(_dot(xn, wg_ref[...]))
        else:
            o = _dot(xn, w_ref[...]) * jnp.where(j < n_scaled, scale, 1.0)
        return o.astype(o_ref.dtype)

    @pl.when(j == 0)
    def _():
        side_casts()
        tm = x_ref.shape[0]
        cr = _tile(tm, NORM_CHUNK_ROWS)
        for c in range(tm // cr):
            r = pl.ds(c * cr, cr)
            xn = _rms_modulate(x_ref[r, :], g_ref[...], mod_ref[0:1, :], mod_ref[1:2, :]).astype(BF16)
            xn_ref[r, :] = xn
            o_ref[r, :] = project(xn)

    @pl.when(j > 0)
    def _():
        side_casts()
        o_ref[...] = project(xn_ref[...])


def _nm_matmul_grid(rows, d, n_out, grp, mode, tm_pref=1024, tn_pref=1024):
    tm = grp.tile(rows, tm_pref)
    tn = _tile(d if mode == "qkv" else n_out, tn_pref)
    return tm, tn, (rows // tm, n_out // tn)


def _cast_block(shape, grid):
    (r, c), (gi, gj) = shape, grid
    if r % gi or c % gj or (r // gi) % (2 * SUBLANES) or (c // gj) % LANES:
        return None
    return r // gi, c // gj


def _nm_matmul(x, g, mods, grp, w, *, mode, out_dtype, casts=(), tm_pref=1024, tn_pref=1024):
    rows, d = x.shape
    n_out = w.shape[1] // 2 if mode == "glu" else w.shape[1]
    tm, tn, grid = _nm_matmul_grid(rows, d, n_out, grp, mode, tm_pref, tn_pref)
    w_specs = [pl.BlockSpec((d, tn), lambda i, j: (0, j))]
    w_args = [w]
    if mode == "glu":
        off = n_out // tn
        w_specs.append(pl.BlockSpec((d, tn), lambda i, j: (0, j + off)))
        w_args.append(w)
    cast_in, cast_out, cast_shape = [], [], []
    for src, layer in casts:
        blk = _cast_block(src.shape[1:], grid)
        cast_in.append(pl.BlockSpec((None,) + blk, lambda i, j, layer=layer: (layer, i, j)))
        cast_out.append(pl.BlockSpec(blk, lambda i, j: (i, j)))
        cast_shape.append(jax.ShapeDtypeStruct(src.shape[1:], BF16))
    kern = functools.partial(_nm_matmul_kernel, mode=mode, n_scaled=d // tn, scale=HEAD_DIM ** -0.5,
                             n_cast=len(casts))
    out = pl.pallas_call(
        kern,
        grid=grid,
        in_specs=[
            pl.BlockSpec((tm, d), lambda i, j: (i, 0)),
            pl.BlockSpec((1, d), lambda i, j: (0, 0)),
            pl.BlockSpec((None, 6, d), lambda i, j: (grp(i, tm), 0, 0)),
        ] + w_specs + cast_in,
        out_specs=[pl.BlockSpec((tm, tn), lambda i, j: (i, j))] + cast_out,
        out_shape=[jax.ShapeDtypeStruct((rows, n_out), out_dtype)] + cast_shape,
        scratch_shapes=[pltpu.VMEM((tm, d), BF16)],
        compiler_params=_params("parallel", "arbitrary"),
        name="nm_matmul_" + mode,
    )(x, g.reshape(1, d), mods, *w_args, *[src for src, _ in casts])
    return out[0], list(out[1:])


class _Group:
    def __init__(self, seq_len, ctx_row=None):
        self.seq_len = seq_len
        self.ctx_row = ctx_row

    def tile(self, rows, pref):
        return _tile(rows if self.ctx_row is not None else self.seq_len, pref)

    def __call__(self, i, tm):
        if self.ctx_row is not None:
            return self.ctx_row
        return (i * tm) // self.seq_len


def _mm_residual_kernel(t_ref, w_ref, h_ref, mod_ref, o_ref):
    o_ref[...] = h_ref[...] + mod_ref[2:3, :] * _dot(t_ref[...], w_ref[...])


def _mm_residual(t, w, h, mods, grp, *, tm_pref=512):
    rows, d = h.shape
    k = t.shape[1]
    tm = grp.tile(rows, tm_pref)
    return pl.pallas_call(
        _mm_residual_kernel,
        grid=(rows // tm,),
        in_specs=[
            pl.BlockSpec((tm, k), lambda i: (i, 0)),
            pl.BlockSpec((k, d), lambda i: (0, 0)),
            pl.BlockSpec((tm, d), lambda i: (i, 0)),
            pl.BlockSpec((None, 6, d), lambda i: (grp(i, tm), 0, 0)),
        ],
        out_specs=pl.BlockSpec((tm, d), lambda i: (i, 0)),
        out_shape=jax.ShapeDtypeStruct((rows, d), F32),
        compiler_params=_params("parallel"),
        name="mm_residual",
    )(t, w, h, mods)


def _ffn_kernel(x_ref, g_ref, mod_ref, w1_ref, w3_ref, w2_ref, gf_ref, *rest, final_norm, n_cast):
    cast_in, o_ref, cast_out, xn_ref = rest[:n_cast], rest[n_cast], rest[n_cast + 1:-1], rest[-1]
    k = pl.program_id(1)
    tm = x_ref.shape[0]

    def side_casts():
        for src, dst in zip(cast_in, cast_out):
            dst[...] = src[...].astype(BF16)

    def row_chunks(rows_per_chunk):
        cr = _tile(tm, rows_per_chunk)
        return [pl.ds(c * cr, cr) for c in range(tm // cr)]

    def partial_out(xn):
        a = _dot(xn, w1_ref[...])
        b = _dot(xn, w3_ref[...])
        return _dot((a * jax.nn.sigmoid(a) * b).astype(BF16), w2_ref[...])

    @pl.when(k == 0)
    def _():
        side_casts()
        for r in row_chunks(NORM_CHUNK_ROWS):
            xn = _rms_modulate(x_ref[r, :], g_ref[...], mod_ref[3:4, :], mod_ref[4:5, :]).astype(BF16)
            xn_ref[r, :] = xn
            o_ref[r, :] = partial_out(xn)

    @pl.when(k > 0)
    def _():
        side_casts()
        for r in row_chunks(FFN_CHUNK_ROWS):
            o_ref[r, :] += partial_out(xn_ref[r, :])

    @pl.when(k == pl.num_programs(1) - 1)
    def _():
        cr = _tile(tm, NORM_CHUNK_ROWS)

        def finish(c, carry):
            r = pl.ds(pl.multiple_of(c * cr, cr), cr)
            y = x_ref[r, :] + mod_ref[5:6, :] * o_ref[r, :]
            if final_norm:
                y = y * lax.rsqrt(jnp.mean(y * y, axis=-1, keepdims=True) + EPS) * gf_ref[...]
            o_ref[r, :] = y
            return carry

        lax.fori_loop(0, tm // cr, finish, 0)


def _ffn_tiles(rows, d, f, grp, tm_pref=1024, tf_pref=512):
    tm = grp.tile(rows, tm_pref)
    return tm, _tile(f, tf_pref)


def _ffn_cast_plan(shape, rows, d, f, grp):
    tm, tf = _ffn_tiles(rows, d, f, grp)
    n_tiles, n_steps = rows // tm, f // tf
    r, c = shape
    aligned = lambda blk: blk[0] % (2 * SUBLANES) == 0 and blk[1] % LANES == 0
    if c == f and r % n_tiles == 0 and aligned((r // n_tiles, tf)):
        return (r // n_tiles, tf), lambda i, k: (i, k)
    if r == f and c % n_tiles == 0 and aligned((tf, c // n_tiles)):
        return (tf, c // n_tiles), lambda i, k: (k, i)
    split = 1
    while split * 2 <= n_steps and _cast_block(shape, (n_tiles, split * 2)):
        split *= 2
    blk = _cast_block(shape, (n_tiles, split))
    return (blk, lambda i, k: (i, jnp.minimum(k, split - 1))) if blk else None


def _ffn(x, g, mods, grp, w1, w3, w2, g_final, *, final_norm, casts=()):
    rows, d = x.shape
    f = w1.shape[1]
    tm, tf = _ffn_tiles(rows, d, f, grp)
    in_specs = [
        pl.BlockSpec((tm, d), lambda i, k: (i, 0)),
        pl.BlockSpec((1, d), lambda i, k: (0, 0)),
        pl.BlockSpec((None, 6, d), lambda i, k: (grp(i, tm), 0, 0)),
        pl.BlockSpec((d, tf), lambda i, k: (0, k)),
        pl.BlockSpec((d, tf), lambda i, k: (0, k)),
        pl.BlockSpec((tf, d), lambda i, k: (k, 0)),
        pl.BlockSpec((1, d), lambda i, k: (0, 0)),
    ]
    args = [x, g.reshape(1, d), mods, w1, w3, w2, g_final.reshape(1, d)]
    out_specs = [pl.BlockSpec((tm, d), lambda i, k: (i, 0))]
    out_shape = [jax.ShapeDtypeStruct((rows, d), F32)]
    for src, layer in casts:
        blk, idx = _ffn_cast_plan(src.shape[1:], rows, d, f, grp)
        in_specs.append(pl.BlockSpec((None,) + blk, lambda i, k, layer=layer, idx=idx: (layer,) + idx(i, k)))
        args.append(src)
        out_specs.append(pl.BlockSpec(blk, idx))
        out_shape.append(jax.ShapeDtypeStruct(src.shape[1:], BF16))
    out = pl.pallas_call(
        functools.partial(_ffn_kernel, final_norm=final_norm, n_cast=len(casts)),
        grid=(rows // tm, f // tf),
        in_specs=in_specs,
        out_specs=out_specs,
        out_shape=out_shape,
        scratch_shapes=[pltpu.VMEM((tm, d), BF16)],
        compiler_params=_params("parallel", "arbitrary"),
        name="ffn",
    )(*args)
    return out[0], list(out[1:])


def _gmlp_kernel(u_ref, v_ref, lng_ref, lnb_ref, ws_ref, bs_ref, wout_ref, h_ref, mod_ref, o_ref,
                 vn_ref, t_ref, *, tm, groups):
    rb = _tile(tm, GMLP_BLOCK_ROWS)
    for b in range(tm // rb):
        rows = pl.ds(b * rb, rb)
        vn_ref[rows, :] = _layernorm(v_ref[rows, :], lng_ref[...], lnb_ref[...]).astype(BF16)
        chunks = [pl.ds(b * rb + c * CHUNK, CHUNK) for c in range(rb // CHUNK)]
        for g in range(groups):
            cols = pl.ds(g * GROUP_DIM, GROUP_DIM)
            sv = _dot(ws_ref[g], jnp.concatenate([vn_ref[r, cols] for r in chunks], axis=1))
            for c, r in enumerate(chunks):
                gate = sv[:, c * GROUP_DIM:(c + 1) * GROUP_DIM] + bs_ref[:, cols]
                t_ref[r, cols] = (u_ref[r, cols] * gate).astype(BF16)
        o_ref[rows, :] = h_ref[rows, :] + mod_ref[2:3, :] * _dot(t_ref[rows, :], wout_ref[...])


def _gmlp_gate_out(z, ln_g, ln_b, w_s, b_full, w_out, h, mods, grp, *, tm_pref=512):
    rows, d = h.shape
    e = z.shape[1] // 2
    groups = e // GROUP_DIM
    tm = grp.tile(rows, tm_pref)
    return pl.pallas_call(
        functools.partial(_gmlp_kernel, tm=tm, groups=groups),
        grid=(rows // tm,),
        in_specs=[
            pl.BlockSpec((tm, e), lambda i: (i, 0)),
            pl.BlockSpec((tm, e), lambda i: (i, 1)),
            pl.BlockSpec((1, e), lambda i: (0, 0)),
            pl.BlockSpec((1, e), lambda i: (0, 0)),
            pl.BlockSpec((groups, CHUNK, CHUNK), lambda i: (0, 0, 0)),
            pl.BlockSpec((CHUNK, e), lambda i: (0, 0)),
            pl.BlockSpec((e, d), lambda i: (0, 0), pipeline_mode=pl.Buffered(1)),
            pl.BlockSpec((tm, d), lambda i: (i, 0)),
            pl.BlockSpec((None, 6, d), lambda i: (grp(i, tm), 0, 0)),
        ],
        out_specs=pl.BlockSpec((tm, d), lambda i: (i, 0)),
        out_shape=jax.ShapeDtypeStruct((rows, d), F32),
        scratch_shapes=[pltpu.VMEM((tm, e), BF16), pltpu.VMEM((tm, e), BF16)],
        compiler_params=_params("parallel"),
        name="gmlp_gate_out",
    )(z, z, ln_g.reshape(1, e), ln_b.reshape(1, e), w_s, b_full, w_out, h, mods)


def _na_bias_plan(rows):
    kr = min(NA_MAX_ROWS, rows)
    nblk = rows // NA_QR
    n_off = 2 * NA_MAX_ROWS - 1
    blocks, block_id = [], []
    for m in (0, 1, nblk - 1):
        base = min(max(NA_QR * m - kr // 2, 0), rows - NA_KR)
        per_row = []
        for ri in range(NA_QR):
            r = NA_QR * m + ri
            r_start = min(max(r - kr // 2, 0), rows - kr)
            pick = [base + ki - r + NA_MAX_ROWS - 1 if r_start <= base + ki < r_start + kr else n_off
                    for ki in range(NA_KR)]
            ids = []
            for j in range(0, NA_KR, NA_PAIR):
                blk = tuple(pick[j:j + NA_PAIR])
                if blk not in blocks:
                    blocks.append(blk)
                ids.append(blocks.index(blk))
            per_row.append(tuple(ids))
        block_id.append(tuple(per_row))
    return n_off, tuple(blocks), tuple(block_id)


def _na_bias_blocks(rpb, blocks, n_off):
    cols = jnp.arange(GRID_W)
    c_start = jnp.clip(cols - NA_COLS // 2, 0, GRID_W - NA_COLS)
    col_ok = (cols[None, :] >= c_start[:, None]) & (cols[None, :] < c_start[:, None] + NA_COLS)
    dc_idx = jnp.clip(cols[None, :] - cols[:, None] + NA_COLS - 1, 0, 2 * NA_COLS - 2)
    slabs = jnp.where(col_ok[None, None], rpb[:, :, dc_idx].astype(F32), NEG_INF)
    assert slabs.shape[1] == n_off
    slabs = jnp.concatenate([slabs, jnp.full_like(slabs[:, :1], NEG_INF)], axis=1)
    parts = [jnp.take(slabs, jnp.asarray([blk[p] for blk in blocks], jnp.int32), axis=1) for p in range(NA_PAIR)]
    return jnp.concatenate(parts, axis=-1)


def _na_kernel(q_ref, k_ref, v_ref, kc_ref, vc_ref, blocks_ref, o_ref, bias_ref, *, rows, heads_per_step, block_id):
    kr = min(NA_MAX_ROWS, rows)
    nblk = rows // NA_QR
    tq = NA_QR * GRID_W

    for variant, per_row in enumerate(block_id):
        for hh in range(heads_per_step):
            for ri, ids in enumerate(per_row):
                for j, b in enumerate(ids):
                    bias_ref[variant, hh, ri * GRID_W:(ri + 1) * GRID_W, j * LANES:(j + 1) * LANES] = blocks_ref[hh, b]

    per_iter = NA_BLOCKS_PER_ITER if nblk % NA_BLOCKS_PER_ITER == 0 else 1

    def q_blocks(it, carry):
        chains = []
        for j in range(per_iter):
            m = it * per_iter + j
            base = jnp.clip(NA_QR * m - kr // 2, 0, rows - NA_KR) * GRID_W
            win = pl.ds(pl.multiple_of(base, tq), NA_KR * GRID_W)
            qrows = pl.ds(pl.multiple_of(m * tq, tq), tq)
            variant = jnp.where(m == 0, 0, jnp.where(m == nblk - 1, 2, 1))
            for hh in range(heads_per_step):
                chains.append((win, qrows, variant, hh, pl.ds(hh * HEAD_DIM, HEAD_DIM)))
        scores = []
        for win, qrows, variant, hh, cols in chains:
            q = q_ref[qrows, cols]
            scores.append((_dot_nt(q, k_ref[win, cols]) + bias_ref[variant, hh], _dot_nt(q, kc_ref[:, cols])))
        probs = []
        for s1, s2 in scores:
            mx = jnp.maximum(jnp.max(s1, axis=-1, keepdims=True), jnp.max(s2, axis=-1, keepdims=True))
            p1 = jnp.exp(s1 - mx)
            p2 = jnp.exp(s2 - mx)
            den = jnp.sum(p1, axis=-1, keepdims=True) + jnp.sum(p2, axis=-1, keepdims=True)
            probs.append((p1.astype(BF16), p2.astype(BF16), den))
        for (win, qrows, _, _, cols), (p1, p2, den) in zip(chains, probs):
            o = _dot(p1, v_ref[win, cols]) + _dot(p2, vc_ref[:, cols])
            o_ref[qrows, cols] = (o / den).astype(o_ref.dtype)
        return carry

    lax.fori_loop(0, nblk // per_iter, q_blocks, 0)


def _na_lat(qkv, qkv_ctx, rpb, bsz, seq_len, ctx_len, *, heads_per_step=2):
    d = qkv.shape[1] // 3
    heads = d // HEAD_DIM
    rows = seq_len // GRID_W
    hp = heads_per_step
    groups = heads // hp
    wide = hp * HEAD_DIM
    n_off, blocks, block_id = _na_bias_plan(rows)
    bias_blocks = _na_bias_blocks(rpb, blocks, n_off)
    return pl.pallas_call(
        functools.partial(_na_kernel, rows=rows, heads_per_step=hp, block_id=block_id),
        grid=(bsz, groups),
        in_specs=[
            pl.BlockSpec((seq_len, wide), lambda b, g: (b, g)),
            pl.BlockSpec((seq_len, wide), lambda b, g: (b, groups + g)),
            pl.BlockSpec((seq_len, wide), lambda b, g: (b, 2 * groups + g)),
            pl.BlockSpec((ctx_len, wide), lambda b, g: (b, groups + g)),
            pl.BlockSpec((ctx_len, wide), lambda b, g: (b, 2 * groups + g)),
            pl.BlockSpec((hp, len(blocks), GRID_W, LANES), lambda b, g: (g, 0, 0, 0)),
        ],
        out_specs=pl.BlockSpec((seq_len, wide), lambda b, g: (b, g)),
        out_shape=jax.ShapeDtypeStruct((bsz * seq_len, d), BF16),
        scratch_shapes=[pltpu.VMEM((len(block_id), hp, NA_QR * GRID_W, NA_KR * GRID_W), F32)],
        compiler_params=_params("parallel", "parallel"),
        name="na_latent",
    )(qkv, qkv, qkv, qkv_ctx, qkv_ctx, bias_blocks)


def _ctx_attn_kernel(q_ref, k_ref, v_ref, o_ref):
    s = _dot_nt(q_ref[...], k_ref[...])
    p = jnp.exp(s - jnp.max(s, axis=-1, keepdims=True))
    den = jnp.sum(p, axis=-1, keepdims=True)
    o_ref[...] = (_dot(p.astype(BF16), v_ref[...]) / den).astype(o_ref.dtype)


def _na_ctx(qkv_ctx, bsz, ctx_len):
    d = qkv_ctx.shape[1] // 3
    heads = d // HEAD_DIM
    return pl.pallas_call(
        _ctx_attn_kernel,
        grid=(bsz, heads),
        in_specs=[
            pl.BlockSpec((ctx_len, HEAD_DIM), lambda b, h: (b, h)),
            pl.BlockSpec((ctx_len, HEAD_DIM), lambda b, h: (b, heads + h)),
            pl.BlockSpec((ctx_len, HEAD_DIM), lambda b, h: (b, 2 * heads + h)),
        ],
        out_specs=pl.BlockSpec((ctx_len, HEAD_DIM), lambda b, h: (b, h)),
        out_shape=jax.ShapeDtypeStruct((bsz * ctx_len, d), BF16),
        compiler_params=_params("parallel", "parallel"),
        name="na_context",
    )(qkv_ctx, qkv_ctx, qkv_ctx)


def _conv_kernel(yp_ref, ym_ref, yn_ref, wdw_ref, bdw_ref, lng_ref, lnb_ref, w2_ref, h_ref, mod_ref, o_ref,
                 win_ref, c_ref, *, tm, tiles_per_seq, width):
    i = pl.program_id(0)
    first = (i % tiles_per_seq) == 0
    last = (i % tiles_per_seq) == tiles_per_seq - 1
    win_ref[0:CONV_HALO, :] = jnp.where(first, 0.0, yp_ref[...])
    win_ref[CONV_HALO:CONV_HALO + tm, :] = ym_ref[...]
    win_ref[CONV_HALO + tm:, :] = jnp.where(last, 0.0, yn_ref[...])
    lead = CONV_HALO - width // 2
    span = CONV_RB + 2 * CONV_HALO

    def col_block(cc, carry):
        cols = pl.ds(pl.multiple_of(cc * LANES, LANES), LANES)
        for rb in range(tm // CONV_RB):
            r0 = rb * CONV_RB
            win = win_ref[r0:r0 + span, cols]
            acc = jnp.broadcast_to(bdw_ref[:, cols], (CONV_RB, LANES))
            for s in range(SUBLANES):
                taps = [k for k in range(width) if (lead + k) % SUBLANES == s]
                shifted = win if s == 0 else pltpu.roll(win, span - s, axis=0)
                for k in taps:
                    a = lead + k - s
                    acc = acc + wdw_ref[k:k + 1, cols] * shifted[a:a + CONV_RB]
            c_ref[r0:r0 + CONV_RB, cols] = acc
        return carry

    lax.fori_loop(0, c_ref.shape[1] // LANES, col_block, 0)
    y = _layernorm(c_ref[...], lng_ref[...], lnb_ref[...])
    y = (y * jax.nn.sigmoid(y)).astype(BF16)
    o_ref[...] = h_ref[...] + mod_ref[2:3, :] * _dot(y, w2_ref[...])


def _conv_out(y, w_dw, b_dw, ln_g, ln_b, w_pw2, h, mods, grp, seq_len, *, tm_pref=256):
    rows, d = h.shape
    width = w_dw.shape[0]
    tm = _tile(seq_len, tm_pref)
    hb = tm // CONV_HALO
    n_halo = rows // CONV_HALO
    wpad = jnp.zeros((8 * pl.cdiv(width, 8), d), F32).at[:width].set(w_dw)
    kern = functools.partial(_conv_kernel, tm=tm, tiles_per_seq=seq_len // tm, width=width)
    return pl.pallas_call(
        kern,
        grid=(rows // tm,),
        in_specs=[
            pl.BlockSpec((CONV_HALO, d), lambda i: (jnp.maximum(i * hb - 1, 0), 0)),
            pl.BlockSpec((tm, d), lambda i: (i, 0)),
            pl.BlockSpec((CONV_HALO, d), lambda i: (jnp.minimum((i + 1) * hb, n_halo - 1), 0)),
            pl.BlockSpec(wpad.shape, lambda i: (0, 0)),
            pl.BlockSpec((1, d), lambda i: (0, 0)),
            pl.BlockSpec((1, d), lambda i: (0, 0)),
            pl.BlockSpec((1, d), lambda i: (0, 0)),
            pl.BlockSpec((d, d), lambda i: (0, 0)),
            pl.BlockSpec((tm, d), lambda i: (i, 0)),
            pl.BlockSpec((None, 6, d), lambda i: (grp(i, tm), 0, 0)),
        ],
        out_specs=pl.BlockSpec((tm, d), lambda i: (i, 0)),
        out_shape=jax.ShapeDtypeStruct((rows, d), F32),
        scratch_shapes=[pltpu.VMEM((tm + 2 * CONV_HALO, d), F32), pltpu.VMEM((tm, d), F32)],
        compiler_params=_params("parallel"),
        name="conv_out",
    )(y, y, y, wpad, b_dw.reshape(1, d), ln_g.reshape(1, d), ln_b.reshape(1, d), w_pw2, h, mods)


def kernel(x, c, ctx, c_ctx, ada_w, ada_b, g_mix, g_ffn, ffn_w1, ffn_w3, ffn_w2,
           a_w_in, a_ln_g, a_ln_b, a_w_s, a_b_s, a_w_out,
           b_w_qkv, b_rpb, b_w_out,
           c_w_pw1, c_w_dw, c_b_dw, c_ln_g, c_ln_b, c_w_pw2, g_final):
    bsz, seq_len, d = x.shape
    ctx_len = ctx.shape[1]
    depth = ada_w.shape[0]
    assert seq_len % (NA_QR * GRID_W) == 0 and seq_len // GRID_W >= NA_KR
    assert seq_len % CHUNK == 0 and ctx_len % CHUNK == 0 and d % LANES == 0
    assert c_w_dw.shape[1] // 2 <= CONV_HALO

    stacked = dict(ffn_w1=ffn_w1, ffn_w3=ffn_w3, ffn_w2=ffn_w2, a_w_in=a_w_in, a_w_out=a_w_out,
                   b_w_qkv=b_w_qkv, b_w_out=b_w_out, c_w_pw1=c_w_pw1, c_w_pw2=c_w_pw2)
    ffn_names = ("ffn_w1", "ffn_w3", "ffn_w2")
    mixer_names = (("a_w_in", "a_w_out"), ("b_w_qkv", "b_w_out"), ("c_w_pw1", "c_w_pw2"))
    ready = {}

    def weight(name, idx):
        if (name, idx) not in ready:
            ready[(name, idx)] = stacked[name][idx].astype(BF16)
        return ready[(name, idx)]

    mods = _ada_mods(c, c_ctx, ada_w, ada_b)
    lat_grp = _Group(seq_len)
    ctx_grp = _Group(ctx_len, ctx_row=bsz)
    streams = [(lat_grp, seq_len), (ctx_grp, ctx_len)]

    h = [x.reshape(bsz * seq_len, d), ctx.reshape(bsz * ctx_len, d)]
    for i in range(depth):
        last = i == depth - 1
        mixer, j = i % N_MIXERS, i // N_MIXERS
        m = mods[i]
        ctx_live = any(l % N_MIXERS == 1 for l in range(i + 1, depth))
        active = [0, 1] if ctx_live else [0]

        if mixer == 0:
            e = a_w_in.shape[2] // 2
            b_full = jnp.broadcast_to(a_b_s[j].T[:, :, None], (CHUNK, e // GROUP_DIM, GROUP_DIM)).reshape(CHUNK, e)
            w_in, w_out, w_s = weight("a_w_in", j), weight("a_w_out", j), a_w_s[j].astype(BF16)
            for s in active:
                grp, _ = streams[s]
                hosted = []
                if i == 0 and s == 0:
                    grid = _nm_matmul_grid(h[s].shape[0], d, 2 * e, grp, "gelu")[2]
                    hosted = [(n, 0) for n in ffn_names if _cast_block(stacked[n].shape[1:], grid)]
                z, copies = _nm_matmul(h[s], g_mix[i], m, grp, w_in, mode="gelu", out_dtype=F32,
                                       casts=[(stacked[n], idx) for n, idx in hosted])
                ready.update(zip(hosted, copies))
                h[s] = _gmlp_gate_out(z, a_ln_g[j], a_ln_b[j], w_s, b_full, w_out, h[s], m, grp)
        elif mixer == 1:
            w_qkv, w_out = weight("b_w_qkv", j), weight("b_w_out", j)
            qkv = [_nm_matmul(h[s], g_mix[i], m, streams[s][0], w_qkv, mode="qkv", out_dtype=BF16)[0]
                   for s in (0, 1)]
            o_lat = _na_lat(qkv[0], qkv[1], b_rpb[j], bsz, seq_len, ctx_len)
            h[0] = _mm_residual(o_lat, w_out, h[0], m, lat_grp)
            if ctx_live:
                o_ctx = _na_ctx(qkv[1], bsz, ctx_len)
                h[1] = _mm_residual(o_ctx, w_out, h[1], m, ctx_grp)
        else:
            w_pw1, w_pw2 = weight("c_w_pw1", j), weight("c_w_pw2", j)
            for s in active:
                grp, slen = streams[s]
                y, _ = _nm_matmul(h[s], g_mix[i], m, grp, w_pw1, mode="glu", out_dtype=F32, tn_pref=512)
                h[s] = _conv_out(y, c_w_dw[j], c_b_dw[j], c_ln_g[j], c_ln_b[j], w_pw2, h[s], m, grp, slen)

        ffn_bf16 = [weight(n, i) for n in ffn_names]
        for s in active:
            grp, _ = streams[s]
            hosted = []
            if s == 0 and not last:
                nxt = i + 1
                wanted = [(n, nxt) for n in ffn_names] + [(n, nxt // N_MIXERS) for n in mixer_names[nxt % N_MIXERS]]
                hosted = [kw for kw in wanted if kw not in ready and
                          _ffn_cast_plan(stacked[kw[0]].shape[1:], h[s].shape[0], d, ffn_w1.shape[2], grp)]
            h[s], copies = _ffn(h[s], g_ffn[i], m, grp, *ffn_bf16, g_final, final_norm=last and s == 0,
                                casts=[(stacked[n], idx) for n, idx in hosted])
            ready.update(zip(hosted, copies))
    return h[0].reshape(bsz, seq_len, d)
```

```python
import functools

import jax
import jax.numpy as jnp
from jax import lax
from jax.experimental import pallas as pl
from jax.experimental.pallas import tpu as pltpu

GRID_W = 64
CHUNK = 128
GROUP_DIM = 128
HEAD_DIM = 128
NA_MAX_ROWS = 8
NA_COLS = 16
N_MIXERS = 3
EPS = 1e-6
NEG_INF = -1e30

NA_QR = 4
NA_KR = NA_QR + NA_MAX_ROWS
NA_BLOCKS_PER_ITER = 2
CONV_HALO = 16
CONV_RB = 64
NORM_CHUNK_ROWS = 256
GMLP_Z_SLOTS = 3
GMLP_BLOCK_ROWS = 256
FFN_CHUNK_ROWS = 256
LANES = 128
SUBLANES = 8
NA_PAIR = LANES // GRID_W
VMEM_LIMIT_BYTES = 60 * 1024 * 1024

F32 = jnp.float32
BF16 = jnp.bfloat16


def _params(*sem):
    return pltpu.CompilerParams(dimension_semantics=sem, vmem_limit_bytes=VMEM_LIMIT_BYTES)


def _tile(n, pref):
    t = min(n, pref)
    while n % t:
        t -= LANES
    return t


def _dot(a, b):
    return jnp.dot(a, b, preferred_element_type=F32)


def _dot_nt(a, b):
    return lax.dot_general(a, b, (((1,), (1,)), ((), ())), preferred_element_type=F32)


def _rms_modulate(x, g, shift, scale):
    gain = g * (1.0 + scale)
    return (x * lax.rsqrt(jnp.mean(x * x, axis=-1, keepdims=True) + EPS)) * gain + shift


def _layernorm(x, g, b):
    mu = jnp.mean(x, axis=-1, keepdims=True)
    xc = x - mu
    var = jnp.mean(xc * xc, axis=-1, keepdims=True)
    return xc * lax.rsqrt(var + EPS) * g + b


def _ada_kernel(s_ref, w_ref, b_ref, o_ref):
    s = s_ref[...]
    a = (s * jax.nn.sigmoid(s)).astype(BF16)
    o_ref[...] = _dot(a, w_ref[...].astype(BF16)) + b_ref[...]


def _ada_mods(c, c_ctx, ada_w, ada_b):
    depth, d, n = ada_w.shape
    bsz = c.shape[0]
    rows = 8 * pl.cdiv(bsz + 1, 8)
    s = jnp.zeros((rows, d), F32).at[:bsz].set(c).at[bsz].set(c_ctx)
    tn = _tile(n, 1024)
    out = pl.pallas_call(
        _ada_kernel,
        grid=(depth, n // tn),
        in_specs=[
            pl.BlockSpec((rows, d), lambda l, j: (0, 0)),
            pl.BlockSpec((None, d, tn), lambda l, j: (l, 0, j)),
            pl.BlockSpec((None, 1, tn), lambda l, j: (l, 0, j)),
        ],
        out_specs=pl.BlockSpec((None, rows, tn), lambda l, j: (l, 0, j)),
        out_shape=jax.ShapeDtypeStruct((depth, rows, n), F32),
        compiler_params=_params("arbitrary", "arbitrary"),
        name="ada_mod",
    )(s, ada_w, ada_b.reshape(depth, 1, n))
    return out[:, :bsz + 1].reshape(depth, bsz + 1, 6, d)


def _nm_matmul_kernel(x_ref, g_ref, mod_ref, *rest, mode, n_scaled, scale, n_cast):
    n_w = 2 if mode == "glu" else 1
    w_refs, cast_in = rest[:n_w], rest[n_w:n_w + n_cast]
    o_ref, cast_out, xn_ref = rest[n_w + n_cast], rest[n_w + n_cast + 1:-1], rest[-1]
    if mode == "glu":
        wa_ref, wg_ref = w_refs
    else:
        w_ref, = w_refs
    j = pl.program_id(1)

    def side_casts():
        for src, dst in zip(cast_in, cast_out):
            dst[...] = src[...].astype(BF16)

    def project(xn):
        if mode == "gelu":
            o = jax.nn.gelu(_dot(xn, w_ref[...]))
        elif mode == "glu":
            o = _dot(xn, wa_ref[...]) * jax.nn.sigmoid(_dot(xn, wg_ref[...]))
        else:
            o = _dot(xn, w_ref[...]) * jnp.where(j < n_scaled, scale, 1.0)
        return o.astype(o_ref.dtype)

    @pl.when(j == 0)
    def _():
        side_casts()
        tm = x_ref.shape[0]
        cr = _tile(tm, NORM_CHUNK_ROWS)
        for c in range(tm // cr):
            r = pl.ds(c * cr, cr)
            xn = _rms_modulate(x_ref[r, :], g_ref[...], mod_ref[0:1, :], mod_ref[1:2, :]).astype(BF16)
            xn_ref[r, :] = xn
            o_ref[r, :] = project(xn)

    @pl.when(j > 0)
    def _():
        side_casts()
        o_ref[...] = project(xn_ref[...])


def _nm_matmul_grid(rows, d, n_out, grp, mode, tm_pref=1024, tn_pref=1024):
    tm = grp.tile(rows, tm_pref)
    tn = _tile(d if mode == "qkv" else n_out, tn_pref)
    return tm, tn, (rows // tm, n_out // tn)


def _cast_block(shape, grid):
    (r, c), (gi, gj) = shape, grid
    if r % gi or c % gj or (r // gi) % (2 * SUBLANES) or (c // gj) % LANES:
        return None
    return r // gi, c // gj


def _nm_matmul(x, g, mods, grp, w, *, mode, out_dtype, casts=(), tm_pref=1024, tn_pref=1024):
    rows, d = x.shape
    n_out = w.shape[1] // 2 if mode == "glu" else w.shape[1]
    tm, tn, grid = _nm_matmul_grid(rows, d, n_out, grp, mode, tm_pref, tn_pref)
    w_specs = [pl.BlockSpec((d, tn), lambda i, j: (0, j))]
    w_args = [w]
    if mode == "glu":
        off = n_out // tn
        w_specs.append(pl.BlockSpec((d, tn), lambda i, j: (0, j + off)))
        w_args.append(w)
    cast_in, cast_out, cast_shape = [], [], []
    for src, layer in casts:
        blk = _cast_block(src.shape[1:], grid)
        cast_in.append(pl.BlockSpec((None,) + blk, lambda i, j, layer=layer: (layer, i, j)))
        cast_out.append(pl.BlockSpec(blk, lambda i, j: (i, j)))
        cast_shape.append(jax.ShapeDtypeStruct(src.shape[1:], BF16))
    kern = functools.partial(_nm_matmul_kernel, mode=mode, n_scaled=d // tn, scale=HEAD_DIM ** -0.5,
                             n_cast=len(casts))
    out = pl.pallas_call(
        kern,
        grid=grid,
        in_specs=[
            pl.BlockSpec((tm, d), lambda i, j: (i, 0)),
            pl.BlockSpec((1, d), lambda i, j: (0, 0)),
            pl.BlockSpec((None, 6, d), lambda i, j: (grp(i, tm), 0, 0)),
        ] + w_specs + cast_in,
        out_specs=[pl.BlockSpec((tm, tn), lambda i, j: (i, j))] + cast_out,
        out_shape=[jax.ShapeDtypeStruct((rows, n_out), out_dtype)] + cast_shape,
        scratch_shapes=[pltpu.VMEM((tm, d), BF16)],
        compiler_params=_params("parallel", "arbitrary"),
        name="nm_matmul_" + mode,
    )(x, g.reshape(1, d), mods, *w_args, *[src for src, _ in casts])
    return out[0], list(out[1:])


class _Group:
    def __init__(self, seq_len, ctx_row=None):
        self.seq_len = seq_len
        self.ctx_row = ctx_row

    def tile(self, rows, pref):
        return _tile(rows if self.ctx_row is not None else self.seq_len, pref)

    def __call__(self, i, tm):
        if self.ctx_row is not None:
            return self.ctx_row
        return (i * tm) // self.seq_len


def _mm_residual_kernel(t_ref, w_ref, h_ref, mod_ref, o_ref):
    o_ref[...] = h_ref[...] + mod_ref[2:3, :] * _dot(t_ref[...], w_ref[...])


def _mm_residual(t, w, h, mods, grp, *, tm_pref=512):
    rows, d = h.shape
    k = t.shape[1]
    tm = grp.tile(rows, tm_pref)
    return pl.pallas_call(
        _mm_residual_kernel,
        grid=(rows // tm,),
        in_specs=[
            pl.BlockSpec((tm, k), lambda i: (i, 0)),
            pl.BlockSpec((k, d), lambda i: (0, 0)),
            pl.BlockSpec((tm, d), lambda i: (i, 0)),
            pl.BlockSpec((None, 6, d), lambda i: (grp(i, tm), 0, 0)),
        ],
        out_specs=pl.BlockSpec((tm, d), lambda i: (i, 0)),
        out_shape=jax.ShapeDtypeStruct((rows, d), F32),
        compiler_params=_params("parallel"),
        name="mm_residual",
    )(t, w, h, mods)


def _ffn_kernel(x_ref, g_ref, mod_ref, w1_ref, w3_ref, w2_ref, gf_ref, *rest, final_norm, n_cast):
    cast_in, o_ref, cast_out, xn_ref = rest[:n_cast], rest[n_cast], rest[n_cast + 1:-1], rest[-1]
    k = pl.program_id(1)
    tm = x_ref.shape[0]

    def side_casts():
        for src, dst in zip(cast_in, cast_out):
            dst[...] = src[...].astype(BF16)

    def row_chunks(rows_per_chunk):
        cr = _tile(tm, rows_per_chunk)
        return [pl.ds(c * cr, cr) for c in range(tm // cr)]

    def partial_out(xn):
        a = _dot(xn, w1_ref[...])
        b = _dot(xn, w3_ref[...])
        return _dot((a * jax.nn.sigmoid(a) * b).astype(BF16), w2_ref[...])

    @pl.when(k == 0)
    def _():
        side_casts()
        for r in row_chunks(NORM_CHUNK_ROWS):
            xn = _rms_modulate(x_ref[r, :], g_ref[...], mod_ref[3:4, :], mod_ref[4:5, :]).astype(BF16)
            xn_ref[r, :] = xn
            o_ref[r, :] = partial_out(xn)

    @pl.when(k > 0)
    def _():
        side_casts()
        for r in row_chunks(FFN_CHUNK_ROWS):
            o_ref[r, :] += partial_out(xn_ref[r, :])

    @pl.when(k == pl.num_programs(1) - 1)
    def _():
        cr = _tile(tm, NORM_CHUNK_ROWS)

        def finish(c, carry):
            r = pl.ds(pl.multiple_of(c * cr, cr), cr)
            y = x_ref[r, :] + mod_ref[5:6, :] * o_ref[r, :]
            if final_norm:
                y = y * lax.rsqrt(jnp.mean(y * y, axis=-1, keepdims=True) + EPS) * gf_ref[...]
            o_ref[r, :] = y
            return carry

        lax.fori_loop(0, tm // cr, finish, 0)


def _ffn_tiles(rows, d, f, grp, tm_pref=1024, tf_pref=512):
    tm = grp.tile(rows, tm_pref)
    return tm, _tile(f, tf_pref)


def _ffn_cast_plan(shape, rows, d, f, grp):
    tm, tf = _ffn_tiles(rows, d, f, grp)
    n_tiles, n_steps = rows // tm, f // tf
    r, c = shape
    aligned = lambda blk: blk[0] % (2 * SUBLANES) == 0 and blk[1] % LANES == 0
    if c == f and r % n_tiles == 0 and aligned((r // n_tiles, tf)):
        return (r // n_tiles, tf), lambda i, k: (i, k)
    if r == f and c % n_tiles == 0 and aligned((tf, c // n_tiles)):
        return (tf, c // n_tiles), lambda i, k: (k, i)
    split = 1
    while split * 2 <= n_steps and _cast_block(shape, (n_tiles, split * 2)):
        split *= 2
    blk = _cast_block(shape, (n_tiles, split))
    return (blk, lambda i, k: (i, jnp.minimum(k, split - 1))) if blk else None


def _ffn(x, g, mods, grp, w1, w3, w2, g_final, *, final_norm, casts=()):
    rows, d = x.shape
    f = w1.shape[1]
    tm, tf = _ffn_tiles(rows, d, f, grp)
    in_specs = [
        pl.BlockSpec((tm, d), lambda i, k: (i, 0)),
        pl.BlockSpec((1, d), lambda i, k: (0, 0)),
        pl.BlockSpec((None, 6, d), lambda i, k: (grp(i, tm), 0, 0)),
        pl.BlockSpec((d, tf), lambda i, k: (0, k)),
        pl.BlockSpec((d, tf), lambda i, k: (0, k)),
        pl.BlockSpec((tf, d), lambda i, k: (k, 0)),
        pl.BlockSpec((1, d), lambda i, k: (0, 0)),
    ]
    args = [x, g.reshape(1, d), mods, w1, w3, w2, g_final.reshape(1, d)]
    out_specs = [pl.BlockSpec((tm, d), lambda i, k: (i, 0))]
    out_shape = [jax.ShapeDtypeStruct((rows, d), F32)]
    for src, layer in casts:
        blk, idx = _ffn_cast_plan(src.shape[1:], rows, d, f, grp)
        in_specs.append(pl.BlockSpec((None,) + blk, lambda i, k, layer=layer, idx=idx: (layer,) + idx(i, k)))
        args.append(src)
        out_specs.append(pl.BlockSpec(blk, idx))
        out_shape.append(jax.ShapeDtypeStruct(src.shape[1:], BF16))
    out = pl.pallas_call(
        functools.partial(_ffn_kernel, final_norm=final_norm, n_cast=len(casts)),
        grid=(rows // tm, f // tf),
        in_specs=in_specs,
        out_specs=out_specs,
        out_shape=out_shape,
        scratch_shapes=[pltpu.VMEM((tm, d), BF16)],
        compiler_params=_params("parallel", "arbitrary"),
        name="ffn",
    )(*args)
    return out[0], list(out[1:])


def _gmlp_kernel(z_hbm, lng_ref, lnb_ref, ws_ref, bs_ref, wout_ref, h_ref, mod_ref, o_ref,
                 zbuf, sem, vn_ref, t_ref, *, tm, groups, n_tiles):
    i = pl.program_id(0)
    e = vn_ref.shape[1]

    def z_copy(step, slot):
        return pltpu.make_async_copy(z_hbm.at[pl.ds(step * tm, tm), :], zbuf.at[slot], sem.at[slot])

    @pl.when(i == 0)
    def _():
        for s in range(min(GMLP_Z_SLOTS - 1, n_tiles)):
            z_copy(s, s).start()

    @pl.when(i + (GMLP_Z_SLOTS - 1) < n_tiles)
    def _():
        ahead = i + (GMLP_Z_SLOTS - 1)
        z_copy(ahead, lax.rem(ahead, GMLP_Z_SLOTS)).start()

    slot = lax.rem(i, GMLP_Z_SLOTS)
    z_copy(i, slot).wait()
    u_ref = zbuf.at[slot, :, pl.ds(0, e)]
    v_ref = zbuf.at[slot, :, pl.ds(e, e)]

    rb = _tile(tm, GMLP_BLOCK_ROWS)
    for b in range(tm // rb):
        rows = pl.ds(b * rb, rb)
        vn_ref[rows, :] = _layernorm(v_ref[rows, :], lng_ref[...], lnb_ref[...]).astype(BF16)
        chunks = [pl.ds(b * rb + c * CHUNK, CHUNK) for c in range(rb // CHUNK)]
        for g in range(groups):
            cols = pl.ds(g * GROUP_DIM, GROUP_DIM)
            sv = _dot(ws_ref[g], jnp.concatenate([vn_ref[r, cols] for r in chunks], axis=1))
            for c, r in enumerate(chunks):
                gate = sv[:, c * GROUP_DIM:(c + 1) * GROUP_DIM] + bs_ref[:, cols]
                t_ref[r, cols] = (u_ref[r, cols] * gate).astype(BF16)
        o_ref[rows, :] = h_ref[rows, :] + mod_ref[2:3, :] * _dot(t_ref[rows, :], wout_ref[...])


def _gmlp_gate_out(z, ln_g, ln_b, w_s, b_full, w_out, h, mods, grp, *, tm_pref=512):
    rows, d = h.shape
    e = z.shape[1] // 2
    groups = e // GROUP_DIM
    tm = grp.tile(rows, tm_pref)
    return pl.pallas_call(
        functools.partial(_gmlp_kernel, tm=tm, groups=groups, n_tiles=rows // tm),
        grid=(rows // tm,),
        in_specs=[
            pl.BlockSpec(memory_space=pl.ANY),
            pl.BlockSpec((1, e), lambda i: (0, 0)),
            pl.BlockSpec((1, e), lambda i: (0, 0)),
            pl.BlockSpec((groups, CHUNK, CHUNK), lambda i: (0, 0, 0)),
            pl.BlockSpec((CHUNK, e), lambda i: (0, 0)),
            pl.BlockSpec((e, d), lambda i: (0, 0), pipeline_mode=pl.Buffered(1)),
            pl.BlockSpec((tm, d), lambda i: (i, 0)),
            pl.BlockSpec((None, 6, d), lambda i: (grp(i, tm), 0, 0)),
        ],
        out_specs=pl.BlockSpec((tm, d), lambda i: (i, 0)),
        out_shape=jax.ShapeDtypeStruct((rows, d), F32),
        scratch_shapes=[pltpu.VMEM((GMLP_Z_SLOTS, tm, 2 * e), F32), pltpu.SemaphoreType.DMA((GMLP_Z_SLOTS,)),
                        pltpu.VMEM((tm, e), BF16), pltpu.VMEM((tm, e), BF16)],
        compiler_params=_params("arbitrary"),
        name="gmlp_gate_out",
    )(z, ln_g.reshape(1, e), ln_b.reshape(1, e), w_s, b_full, w_out, h, mods)


def _na_bias_plan(rows):
    kr = min(NA_MAX_ROWS, rows)
    nblk = rows // NA_QR
    n_off = 2 * NA_MAX_ROWS - 1
    blocks, block_id = [], []
    for m in (0, 1, nblk - 1):
        base = min(max(NA_QR * m - kr // 2, 0), rows - NA_KR)
        per_row = []
        for ri in range(NA_QR):
            r = NA_QR * m + ri
            r_start = min(max(r - kr // 2, 0), rows - kr)
            pick = [base + ki - r + NA_MAX_ROWS - 1 if r_start <= base + ki < r_start + kr else n_off
                    for ki in range(NA_KR)]
            ids = []
            for j in range(0, NA_KR, NA_PAIR):
                blk = tuple(pick[j:j + NA_PAIR])
                if blk not in blocks:
                    blocks.append(blk)
                ids.append(blocks.index(blk))
            per_row.append(tuple(ids))
        block_id.append(tuple(per_row))
    return n_off, tuple(blocks), tuple(block_id)


def _na_bias_blocks(rpb, blocks, n_off):
    cols = jnp.arange(GRID_W)
    c_start = jnp.clip(cols - NA_COLS // 2, 0, GRID_W - NA_COLS)
    col_ok = (cols[None, :] >= c_start[:, None]) & (cols[None, :] < c_start[:, None] + NA_COLS)
    dc_idx = jnp.clip(cols[None, :] - cols[:, None] + NA_COLS - 1, 0, 2 * NA_COLS - 2)
    slabs = jnp.where(col_ok[None, None], rpb[:, :, dc_idx].astype(F32), NEG_INF)
    assert slabs.shape[1] == n_off
    slabs = jnp.concatenate([slabs, jnp.full_like(slabs[:, :1], NEG_INF)], axis=1)
    parts = [jnp.take(slabs, jnp.asarray([blk[p] for blk in blocks], jnp.int32), axis=1) for p in range(NA_PAIR)]
    return jnp.concatenate(parts, axis=-1)


def _na_kernel(q_ref, k_ref, v_ref, kc_ref, vc_ref, blocks_ref, o_ref, bias_ref, *, rows, heads_per_step, block_id):
    kr = min(NA_MAX_ROWS, rows)
    nblk = rows // NA_QR
    tq = NA_QR * GRID_W

    for variant, per_row in enumerate(block_id):
        for hh in range(heads_per_step):
            for ri, ids in enumerate(per_row):
                for j, b in enumerate(ids):
                    bias_ref[variant, hh, ri * GRID_W:(ri + 1) * GRID_W, j * LANES:(j + 1) * LANES] = blocks_ref[hh, b]

    per_iter = NA_BLOCKS_PER_ITER if nblk % NA_BLOCKS_PER_ITER == 0 else 1

    def q_blocks(it, carry):
        chains = []
        for j in range(per_iter):
            m = it * per_iter + j
            base = jnp.clip(NA_QR * m - kr // 2, 0, rows - NA_KR) * GRID_W
            win = pl.ds(pl.multiple_of(base, tq), NA_KR * GRID_W)
            qrows = pl.ds(pl.multiple_of(m * tq, tq), tq)
            variant = jnp.where(m == 0, 0, jnp.where(m == nblk - 1, 2, 1))
            for hh in range(heads_per_step):
                chains.append((win, qrows, variant, hh, pl.ds(hh * HEAD_DIM, HEAD_DIM)))
        scores = []
        for win, qrows, variant, hh, cols in chains:
            q = q_ref[qrows, cols]
            scores.append((_dot_nt(q, k_ref[win, cols]) + bias_ref[variant, hh], _dot_nt(q, kc_ref[:, cols])))
        probs = []
        for s1, s2 in scores:
            mx = jnp.maximum(jnp.max(s1, axis=-1, keepdims=True), jnp.max(s2, axis=-1, keepdims=True))
            p1 = jnp.exp(s1 - mx)
            p2 = jnp.exp(s2 - mx)
            den = jnp.sum(p1, axis=-1, keepdims=True) + jnp.sum(p2, axis=-1, keepdims=True)
            probs.append((p1.astype(BF16), p2.astype(BF16), den))
        for (win, qrows, _, _, cols), (p1, p2, den) in zip(chains, probs):
            o = _dot(p1, v_ref[win, cols]) + _dot(p2, vc_ref[:, cols])
            o_ref[qrows, cols] = (o / den).astype(o_ref.dtype)
        return carry

    lax.fori_loop(0, nblk // per_iter, q_blocks, 0)


def _na_lat(qkv, qkv_ctx, rpb, bsz, seq_len, ctx_len, *, heads_per_step=2):
    d = qkv.shape[1] // 3
    heads = d // HEAD_DIM
    rows = seq_len // GRID_W
    hp = heads_per_step
    groups = heads // hp
    wide = hp * HEAD_DIM
    n_off, blocks, block_id = _na_bias_plan(rows)
    bias_blocks = _na_bias_blocks(rpb, blocks, n_off)
    return pl.pallas_call(
        functools.partial(_na_kernel, rows=rows, heads_per_step=hp, block_id=block_id),
        grid=(bsz, groups),
        in_specs=[
            pl.BlockSpec((seq_len, wide), lambda b, g: (b, g)),
            pl.BlockSpec((seq_len, wide), lambda b, g: (b, groups + g)),
            pl.BlockSpec((seq_len, wide), lambda b, g: (b, 2 * groups + g)),
            pl.BlockSpec((ctx_len, wide), lambda b, g: (b, groups + g)),
            pl.BlockSpec((ctx_len, wide), lambda b, g: (b, 2 * groups + g)),
            pl.BlockSpec((hp, len(blocks), GRID_W, LANES), lambda b, g: (g, 0, 0, 0)),
        ],
        out_specs=pl.BlockSpec((seq_len, wide), lambda b, g: (b, g)),
        out_shape=jax.ShapeDtypeStruct((bsz * seq_len, d), BF16),
        scratch_shapes=[pltpu.VMEM((len(block_id), hp, NA_QR * GRID_W, NA_KR * GRID_W), F32)],
        compiler_params=_params("parallel", "parallel"),
        name="na_latent",
    )(qkv, qkv, qkv, qkv_ctx, qkv_ctx, bias_blocks)


def _ctx_attn_kernel(q_ref, k_ref, v_ref, o_ref):
    s = _dot_nt(q_ref[...], k_ref[...])
    p = jnp.exp(s - jnp.max(s, axis=-1, keepdims=True))
    den = jnp.sum(p, axis=-1, keepdims=True)
    o_ref[...] = (_dot(p.astype(BF16), v_ref[...]) / den).astype(o_ref.dtype)


def _na_ctx(qkv_ctx, bsz, ctx_len):
    d = qkv_ctx.shape[1] // 3
    heads = d // HEAD_DIM
    return pl.pallas_call(
        _ctx_attn_kernel,
        grid=(bsz, heads),
        in_specs=[
            pl.BlockSpec((ctx_len, HEAD_DIM), lambda b, h: (b, h)),
            pl.BlockSpec((ctx_len, HEAD_DIM), lambda b, h: (b, heads + h)),
            pl.BlockSpec((ctx_len, HEAD_DIM), lambda b, h: (b, 2 * heads + h)),
        ],
        out_specs=pl.BlockSpec((ctx_len, HEAD_DIM), lambda b, h: (b, h)),
        out_shape=jax.ShapeDtypeStruct((bsz * ctx_len, d), BF16),
        compiler_params=_params("parallel", "parallel"),
        name="na_context",
    )(qkv_ctx, qkv_ctx, qkv_ctx)


def _conv_kernel(yp_ref, ym_ref, yn_ref, wdw_ref, bdw_ref, lng_ref, lnb_ref, w2_ref, h_ref, mod_ref, o_ref,
                 win_ref, c_ref, *, tm, tiles_per_seq, width):
    i = pl.program_id(0)
    first = (i % tiles_per_seq) == 0
    last = (i % tiles_per_seq) == tiles_per_seq - 1
    win_ref[0:CONV_HALO, :] = jnp.where(first, 0.0, yp_ref[...])
    win_ref[CONV_HALO:CONV_HALO + tm, :] = ym_ref[...]
    win_ref[CONV_HALO + tm:, :] = jnp.where(last, 0.0, yn_ref[...])
    lead = CONV_HALO - width // 2
    span = CONV_RB + 2 * CONV_HALO

    def col_block(cc, carry):
        cols = pl.ds(pl.multiple_of(cc * LANES, LANES), LANES)
        for rb in range(tm // CONV_RB):
            r0 = rb * CONV_RB
            win = win_ref[r0:r0 + span, cols]
            acc = jnp.broadcast_to(bdw_ref[:, cols], (CONV_RB, LANES))
            for s in range(SUBLANES):
                taps = [k for k in range(width) if (lead + k) % SUBLANES == s]
                shifted = win if s == 0 else pltpu.roll(win, span - s, axis=0)
                for k in taps:
                    a = lead + k - s
                    acc = acc + wdw_ref[k:k + 1, cols] * shifted[a:a + CONV_RB]
            c_ref[r0:r0 + CONV_RB, cols] = acc
        return carry

    lax.fori_loop(0, c_ref.shape[1] // LANES, col_block, 0)
    y = _layernorm(c_ref[...], lng_ref[...], lnb_ref[...])
    y = (y * jax.nn.sigmoid(y)).astype(BF16)
    o_ref[...] = h_ref[...] + mod_ref[2:3, :] * _dot(y, w2_ref[...])


def _conv_out(y, w_dw, b_dw, ln_g, ln_b, w_pw2, h, mods, grp, seq_len, *, tm_pref=256):
    rows, d = h.shape
    width = w_dw.shape[0]
    tm = _tile(seq_len, tm_pref)
    hb = tm // CONV_HALO
    n_halo = rows // CONV_HALO
    wpad = jnp.zeros((8 * pl.cdiv(width, 8), d), F32).at[:width].set(w_dw)
    kern = functools.partial(_conv_kernel, tm=tm, tiles_per_seq=seq_len // tm, width=width)
    return pl.pallas_call(
        kern,
        grid=(rows // tm,),
        in_specs=[
            pl.BlockSpec((CONV_HALO, d), lambda i: (jnp.maximum(i * hb - 1, 0), 0)),
            pl.BlockSpec((tm, d), lambda i: (i, 0)),
            pl.BlockSpec((CONV_HALO, d), lambda i: (jnp.minimum((i + 1) * hb, n_halo - 1), 0)),
            pl.BlockSpec(wpad.shape, lambda i: (0, 0)),
            pl.BlockSpec((1, d), lambda i: (0, 0)),
            pl.BlockSpec((1, d), lambda i: (0, 0)),
            pl.BlockSpec((1, d), lambda i: (0, 0)),
            pl.BlockSpec((d, d), lambda i: (0, 0)),
            pl.BlockSpec((tm, d), lambda i: (i, 0)),
            pl.BlockSpec((None, 6, d), lambda i: (grp(i, tm), 0, 0)),
        ],
        out_specs=pl.BlockSpec((tm, d), lambda i: (i, 0)),
        out_shape=jax.ShapeDtypeStruct((rows, d), F32),
        scratch_shapes=[pltpu.VMEM((tm + 2 * CONV_HALO, d), F32), pltpu.VMEM((tm, d), F32)],
        compiler_params=_params("parallel"),
        name="conv_out",
    )(y, y, y, wpad, b_dw.reshape(1, d), ln_g.reshape(1, d), ln_b.reshape(1, d), w_pw2, h, mods)


def kernel(x, c, ctx, c_ctx, ada_w, ada_b, g_mix, g_ffn, ffn_w1, ffn_w3, ffn_w2,
           a_w_in, a_ln_g, a_ln_b, a_w_s, a_b_s, a_w_out,
           b_w_qkv, b_rpb, b_w_out,
           c_w_pw1, c_w_dw, c_b_dw, c_ln_g, c_ln_b, c_w_pw2, g_final):
    bsz, seq_len, d = x.shape
    ctx_len = ctx.shape[1]
    depth = ada_w.shape[0]
    assert seq_len % (NA_QR * GRID_W) == 0 and seq_len // GRID_W >= NA_KR
    assert seq_len % CHUNK == 0 and ctx_len % CHUNK == 0 and d % LANES == 0
    assert c_w_dw.shape[1] // 2 <= CONV_HALO

    stacked = dict(ffn_w1=ffn_w1, ffn_w3=ffn_w3, ffn_w2=ffn_w2, a_w_in=a_w_in, a_w_out=a_w_out,
                   b_w_qkv=b_w_qkv, b_w_out=b_w_out, c_w_pw1=c_w_pw1, c_w_pw2=c_w_pw2)
    ffn_names = ("ffn_w1", "ffn_w3", "ffn_w2")
    mixer_names = (("a_w_in", "a_w_out"), ("b_w_qkv", "b_w_out"), ("c_w_pw1", "c_w_pw2"))
    ready = {}

    def weight(name, idx):
        if (name, idx) not in ready:
            ready[(name, idx)] = stacked[name][idx].astype(BF16)
        return ready[(name, idx)]

    mods = _ada_mods(c, c_ctx, ada_w, ada_b)
    lat_grp = _Group(seq_len)
    ctx_grp = _Group(ctx_len, ctx_row=bsz)
    streams = [(lat_grp, seq_len), (ctx_grp, ctx_len)]

    h = [x.reshape(bsz * seq_len, d), ctx.reshape(bsz * ctx_len, d)]
    for i in range(depth):
        last = i == depth - 1
        mixer, j = i % N_MIXERS, i // N_MIXERS
        m = mods[i]
        ctx_live = any(l % N_MIXERS == 1 for l in range(i + 1, depth))
        active = [0, 1] if ctx_live else [0]

        if mixer == 0:
            e = a_w_in.shape[2] // 2
            b_full = jnp.broadcast_to(a_b_s[j].T[:, :, None], (CHUNK, e // GROUP_DIM, GROUP_DIM)).reshape(CHUNK, e)
            w_in, w_out, w_s = weight("a_w_in", j), weight("a_w_out", j), a_w_s[j].astype(BF16)
            for s in active:
                grp, _ = streams[s]
                hosted = []
                if i == 0 and s == 0:
                    grid = _nm_matmul_grid(h[s].shape[0], d, 2 * e, grp, "gelu")[2]
                    hosted = [(n, 0) for n in ffn_names if _cast_block(stacked[n].shape[1:], grid)]
                z, copies = _nm_matmul(h[s], g_mix[i], m, grp, w_in, mode="gelu", out_dtype=F32,
                                       casts=[(stacked[n], idx) for n, idx in hosted])
                ready.update(zip(hosted, copies))
                h[s] = _gmlp_gate_out(z, a_ln_g[j], a_ln_b[j], w_s, b_full, w_out, h[s], m, grp)
        elif mixer == 1:
            w_qkv, w_out = weight("b_w_qkv", j), weight("b_w_out", j)
            qkv = [_nm_matmul(h[s], g_mix[i], m, streams[s][0], w_qkv, mode="qkv", out_dtype=BF16)[0]
                   for s in (0, 1)]
            o_lat = _na_lat(qkv[0], qkv[1], b_rpb[j], bsz, seq_len, ctx_len)
            h[0] = _mm_residual(o_lat, w_out, h[0], m, lat_grp)
            if ctx_live:
                o_ctx = _na_ctx(qkv[1], bsz, ctx_len)
                h[1] = _mm_residual(o_ctx, w_out, h[1], m, ctx_grp)
        else:
            w_pw1, w_pw2 = weight("c_w_pw1", j), weight("c_w_pw2", j)
            for s in active:
                grp, slen = streams[s]
                y, _ = _nm_matmul(h[s], g_mix[i], m, grp, w_pw1, mode="glu", out_dtype=F32, tn_pref=512)
                h[s] = _conv_out(y, c_w_dw[j], c_b_dw[j], c_ln_g[j], c_ln_b[j], w_pw2, h[s], m, grp, slen)

        ffn_bf16 = [weight(n, i) for n in ffn_names]
        for s in active:
            grp, _ = streams[s]
            hosted = []
            if s == 0 and not last:
                nxt = i + 1
                wanted = [(n, nxt) for n in ffn_names] + [(n, nxt // N_MIXERS) for n in mixer_names[nxt % N_MIXERS]]
                hosted = [kw for kw in wanted if kw not in ready and
                          _ffn_cast_plan(stacked[kw[0]].shape[1:], h[s].shape[0], d, ffn_w1.shape[2], grp)]
            h[s], copies = _ffn(h[s], g_ffn[i], m, grp, *ffn_bf16, g_final, final_norm=last and s == 0,
                                casts=[(stacked[n], idx) for n, idx in hosted])
            ready.update(zip(hosted, copies))
    return h[0].reshape(bsz, seq_len, d)
```
